```python
import math
import jax, jax.numpy as jnp
from jax import lax
import numpy as np

D_MODEL = 1024
BATCH = 4
SEQ = 8192
DEPTH = 2

N_MIXERS = 2
S5_GROUP = 16
S5_GROUPS = D_MODEL // S5_GROUP
S5_STATE = 64
SCAN_CHUNK = 128
DT_MIN = 1e-3
DT_MAX = 1e-1
CONV_WIDTH = 3
D_FF = ((11 * D_MODEL // 4 + 127) // 128) * 128
N_A_LAYERS = (DEPTH + 1) // 2
N_B_LAYERS = DEPTH // 2
RMS_EPS = 1e-6

kernel_name = "hybrid_s5_shortconv_convffn"


def rmsnorm(x, g):
    xf = x.astype(jnp.float32)
    y = xf * lax.rsqrt(jnp.mean(xf * xf, axis=-1, keepdims=True) + RMS_EPS)
    return (y * g.astype(jnp.float32)).astype(x.dtype)


def causal_dwconv(x, w):
    k_w = w.shape[0]
    seq = x.shape[1]
    xp = jnp.pad(x, ((0, 0), (k_w - 1, 0), (0, 0)))
    y = xp[:, 0:seq] * w[0]
    for k in range(1, k_w):
        y = y + xp[:, k:k + seq] * w[k]
    return y


def _ssm_combine(left, right):
    ar_l, ai_l, br_l, bi_l = left
    ar_r, ai_r, br_r, bi_r = right
    return (ar_r * ar_l - ai_r * ai_l,
            ar_r * ai_l + ai_r * ar_l,
            ar_r * br_l - ai_r * bi_l + br_r,
            ar_r * bi_l + ai_r * br_l + bi_r)


def s5_mixer(u, a_re, a_im, log_dt, b_re, b_im, c_re, c_im, d_skip, w_glu):
    f32 = jnp.float32
    bsz, seq, _ = u.shape
    lam_r = a_re.astype(f32)
    lam_i = a_im.astype(f32)
    dt = jnp.exp(log_dt.astype(f32))[:, None]
    mag = jnp.exp(lam_r * dt)
    ab_r = mag * jnp.cos(lam_i * dt)
    ab_i = mag * jnp.sin(lam_i * dt)
    den = lam_r * lam_r + lam_i * lam_i
    nr = ab_r - 1.0
    g_r = ((nr * lam_r + ab_i * lam_i) / den)[..., None]
    g_i = ((ab_i * lam_r - nr * lam_i) / den)[..., None]
    br = b_re.astype(f32)
    bi = b_im.astype(f32)
    bb_r = g_r * br - g_i * bi
    bb_i = g_r * bi + g_i * br
    cr = c_re.astype(f32)
    ci = c_im.astype(f32)
    steps = jnp.arange(1, SCAN_CHUNK + 1, dtype=f32)[:, None, None]
    pmag = jnp.exp(lam_r * dt * steps)
    pw_r = pmag * jnp.cos(lam_i * dt * steps)
    pw_i = pmag * jnp.sin(lam_i * dt * steps)
    a_blk_r = jnp.broadcast_to(ab_r, (bsz, SCAN_CHUNK, S5_GROUPS, S5_STATE))
    a_blk_i = jnp.broadcast_to(ab_i, (bsz, SCAN_CHUNK, S5_GROUPS, S5_STATE))

    n_chunks = seq // SCAN_CHUNK
    uf = u.astype(f32)
    uc = uf.reshape(bsz, n_chunks, SCAN_CHUNK, S5_GROUPS, S5_GROUP).transpose(1, 0, 2, 3, 4)

    def chunk_step(carry, u_blk):
        h0_r, h0_i = carry
        bu_r = jnp.einsum('btgh,gph->btgp', u_blk, bb_r)
        bu_i = jnp.einsum('btgh,gph->btgp', u_blk, bb_i)
        _, _, loc_r, loc_i = lax.associative_scan(
            _ssm_combine, (a_blk_r, a_blk_i, bu_r, bu_i), axis=1)
        h_r = loc_r + pw_r * h0_r[:, None] - pw_i * h0_i[:, None]
        h_i = loc_i + pw_r * h0_i[:, None] + pw_i * h0_r[:, None]
        y = jnp.einsum('btgp,ghp->btgh', h_r, cr) - jnp.einsum('btgp,ghp->btgh', h_i, ci)
        return (h_r[:, -1], h_i[:, -1]), y

    h_init = (jnp.zeros((bsz, S5_GROUPS, S5_STATE), f32),
              jnp.zeros((bsz, S5_GROUPS, S5_STATE), f32))
    _, ys = lax.scan(chunk_step, h_init, uc)
    y = ys.transpose(1, 0, 2, 3, 4).reshape(bsz, seq, D_MODEL)
    y = y + d_skip.astype(f32) * uf
    z = jax.nn.gelu(y)
    za, zg = jnp.split(z @ w_glu.astype(f32), 2, axis=-1)
    return (za * jax.nn.sigmoid(zg)).astype(u.dtype)


def shortconv_mixer(u, w_in, conv_w, w_out):
    b_gate, c_gate, h = jnp.split(u @ w_in, 3, axis=-1)
    v = causal_dwconv(c_gate * h, conv_w)
    return (b_gate * v) @ w_out


def conv_ffn(u, w_up, conv_w, conv_b, w_down):
    g, v = jnp.split(u @ w_up, 2, axis=-1)
    g = causal_dwconv(g, conv_w) + conv_b
    return (jax.nn.silu(g) * v) @ w_down


def setup_inputs(seed: int = 0) -> dict:
    key = jax.random.key(seed)
    ks = jax.random.split(key, 24)
    f32 = jnp.float32
    d = D_MODEL
    g, p, h, f = S5_GROUPS, S5_STATE, S5_GROUP, D_FF
    na, nb = N_A_LAYERS, N_B_LAYERS
    nrm = lambda k, s, sc: jax.random.normal(k, s, f32) * sc
    a_im_base = jnp.arange(p, dtype=f32) * jnp.pi
    return {
        "x": nrm(ks[0], (BATCH, SEQ, d), 1.0),
        "norm_mix": 1.0 + nrm(ks[1], (DEPTH, d), 0.02),
        "norm_ffn": 1.0 + nrm(ks[2], (DEPTH, d), 0.02),
        "norm_final": 1.0 + nrm(ks[3], (d,), 0.02),
        "s5_a_re": -0.5 + nrm(ks[4], (na, g, p), 0.01),
        "s5_a_im": a_im_base + nrm(ks[5], (na, g, p), 0.01),
        "s5_log_dt": jax.random.uniform(ks[6], (na, g), f32,
                                        minval=math.log(DT_MIN), maxval=math.log(DT_MAX)),
        "s5_b_re": nrm(ks[7], (na, g, p, h), (2.0 * h) ** -0.5),
        "s5_b_im": nrm(ks[8], (na, g, p, h), (2.0 * h) ** -0.5),
        "s5_c_re": nrm(ks[9], (na, g, h, p), (1.0 * p) ** -0.5),
        "s5_c_im": nrm(ks[10], (na, g, h, p), (1.0 * p) ** -0.5),
        "s5_d": nrm(ks[11], (na, d), 1.0),
        "s5_w_glu": nrm(ks[12], (na, d, 2 * d), d ** -0.5),
        "sc_w_in": nrm(ks[13], (nb, d, 3 * d), d ** -0.5),
        "sc_conv_w": nrm(ks[14], (nb, CONV_WIDTH, d), CONV_WIDTH ** -0.5),
        "sc_w_out": nrm(ks[15], (nb, d, d), d ** -0.5),
        "ffn_w_up": nrm(ks[16], (DEPTH, d, 2 * f), d ** -0.5),
        "ffn_conv_w": nrm(ks[17], (DEPTH, CONV_WIDTH, f), CONV_WIDTH ** -0.5),
        "ffn_conv_b": nrm(ks[18], (DEPTH, f), 0.01),
        "ffn_w_down": nrm(ks[19], (DEPTH, f, d), f ** -0.5),
    }


def reference(x, norm_mix, norm_ffn, norm_final, s5_a_re, s5_a_im, s5_log_dt,
              s5_b_re, s5_b_im, s5_c_re, s5_c_im, s5_d, s5_w_glu,
              sc_w_in, sc_conv_w, sc_w_out,
              ffn_w_up, ffn_conv_w, ffn_conv_b, ffn_w_down):
    h = x
    for i in range(DEPTH):
        j = i // N_MIXERS
        u = rmsnorm(h, norm_mix[i])
        if i % N_MIXERS == 0:
            mix = s5_mixer(u, s5_a_re[j], s5_a_im[j], s5_log_dt[j], s5_b_re[j], s5_b_im[j],
                           s5_c_re[j], s5_c_im[j], s5_d[j], s5_w_glu[j])
        else:
            mix = shortconv_mixer(u, sc_w_in[j], sc_conv_w[j], sc_w_out[j])
        h = h + mix.astype(h.dtype)
        u = rmsnorm(h, norm_ffn[i])
        h = h + conv_ffn(u, ffn_w_up[i], ffn_conv_w[i], ffn_conv_b[i], ffn_w_down[i]).astype(h.dtype)
    return rmsnorm(h, norm_final)
```

```python
import functools

import jax
import jax.numpy as jnp
from jax import lax
from jax.experimental import pallas as pl
from jax.experimental.pallas import tpu as pltpu

RMS_EPS = 1e-6
S5_GROUP = 16
S5_STATE = 64
CONV_WIDTH = 3
HALO_ROWS = 8

SCAN_ROWS = 128
GROUP_BLOCK_CH = 256
VMEM_LIMIT_BYTES = 56 * 1024 * 1024


def _rmsnorm(x, gain):
    ms = jnp.mean(x * x, axis=-1, keepdims=True)
    return x * lax.rsqrt(ms + RMS_EPS) * gain


def _const_spec(shape):
    nd = len(shape)
    return pl.BlockSpec(shape, lambda b, i: (0,) * nd, pipeline_mode=pl.Buffered(1))


def _tile_spec(tile_rows, d):
    return pl.BlockSpec((None, tile_rows, d), lambda b, i: (b, i, 0))


def _compiler_params():
    return pltpu.CompilerParams(
        dimension_semantics=("arbitrary", "arbitrary"),
        vmem_limit_bytes=VMEM_LIMIT_BYTES,
    )


def _s5_kernel(h_ref, gain_ref, er_ref, ei_ref, fr_ref, fi_ref, gr_ref, gi_ref,
               tri_ref, wb_ref, wc_ref, dskip_ref, wglu_ref, o_ref,
               carry_ref, hstate_ref, y_ref, *, tile_rows, n_blocks, modes):
    @pl.when(pl.program_id(1) == 0)
    def _():
        carry_ref[...] = jnp.zeros_like(carry_ref)

    x = h_ref[...]
    u = _rmsnorm(x, gain_ref[...])
    u_bf = u.astype(jnp.bfloat16)
    tri = tri_ref[...]
    n_chunks = tile_rows // SCAN_ROWS

    for gb in range(n_blocks):
        ch = slice(gb * GROUP_BLOCK_CH, (gb + 1) * GROUP_BLOCK_CH)
        md = slice(gb * modes, (gb + 1) * modes)
        bu = jnp.dot(u_bf[:, ch], wb_ref[gb], preferred_element_type=jnp.float32)
        er, ei = er_ref[:, md], ei_ref[:, md]
        fr, fi = fr_ref[:, md], fi_ref[:, md]
        gr, gi = gr_ref[:, md], gi_ref[:, md]
        for c in range(n_chunks):
            rows = slice(c * SCAN_ROWS, (c + 1) * SCAN_ROWS)
            bur = bu[rows, :modes]
            bui = bu[rows, modes:]
            zr = er * bur - ei * bui
            zi = er * bui + ei * bur
            z = jnp.concatenate([zr, zi], axis=1).astype(jnp.bfloat16)
            xs = jnp.dot(tri, z, preferred_element_type=jnp.float32)
            xr = xs[:, :modes] + carry_ref[0:1, md]
            xi = xs[:, modes:] + carry_ref[1:2, md]
            hr = fr * xr - fi * xi
            hi = fr * xi + fi * xr
            hstate_ref[rows, :modes] = hr.astype(jnp.bfloat16)
            hstate_ref[rows, modes:] = hi.astype(jnp.bfloat16)
            hpr = hr[SCAN_ROWS - 1:SCAN_ROWS, :]
            hpi = hi[SCAN_ROWS - 1:SCAN_ROWS, :]
            carry_ref[0:1, md] = gr * hpr - gi * hpi
            carry_ref[1:2, md] = gr * hpi + gi * hpr
        y_ref[:, ch] = jnp.dot(hstate_ref[...], wc_ref[gb], preferred_element_type=jnp.float32)

    y = y_ref[...] + dskip_ref[...] * u
    z = jax.nn.gelu(y).astype(jnp.bfloat16)
    zz = jnp.dot(z, wglu_ref[...], preferred_element_type=jnp.float32)
    d = x.shape[-1]
    o_ref[...] = x + zz[:, :d] * jax.nn.sigmoid(zz[:, d:])


def _s5_layer(h, gain, tabs, tri, wb, wc, dskip, wglu, *, tile_rows):
    bsz, seq, d = h.shape
    n_blocks = d // GROUP_BLOCK_CH
    modes = (GROUP_BLOCK_CH // S5_GROUP) * S5_STATE
    er, ei, fr, fi, gr, gi = tabs
    kernel = functools.partial(_s5_kernel, tile_rows=tile_rows, n_blocks=n_blocks, modes=modes)
    return pl.pallas_call(
        kernel,
        grid=(bsz, seq // tile_rows),
        in_specs=[
            _tile_spec(tile_rows, d),
            _const_spec(gain.shape),
            _const_spec(er.shape), _const_spec(ei.shape),
            _const_spec(fr.shape), _const_spec(fi.shape),
            _const_spec(gr.shape), _const_spec(gi.shape),
            _const_spec(tri.shape),
            _const_spec(wb.shape), _const_spec(wc.shape),
            _const_spec(dskip.shape), _const_spec(wglu.shape),
        ],
        out_specs=_tile_spec(tile_rows, d),
        out_shape=jax.ShapeDtypeStruct(h.shape, h.dtype),
        scratch_shapes=[
            pltpu.VMEM((HALO_ROWS, n_blocks * modes), jnp.float32),
            pltpu.VMEM((tile_rows, 2 * modes), jnp.bfloat16),
            pltpu.VMEM((tile_rows, d), jnp.float32),
        ],
        compiler_params=_compiler_params(),
        name="s5_mixer",
    )(h, gain, er, ei, fr, fi, gr, gi, tri, wb, wc, dskip, wglu)


def _causal_conv_rows(val, buf_ref, w_ref, tile_rows):
    buf_ref[HALO_ROWS:HALO_ROWS + tile_rows, :] = val
    out = val * w_ref[CONV_WIDTH - 1:CONV_WIDTH, :]
    for k in range(CONV_WIDTH - 1):
        shift = CONV_WIDTH - 1 - k
        out = out + buf_ref[HALO_ROWS - shift:HALO_ROWS - shift + tile_rows, :] * w_ref[k:k + 1, :]
    buf_ref[0:HALO_ROWS, :] = buf_ref[tile_rows:tile_rows + HALO_ROWS, :]
    return out


def _zero_halo_at_sequence_start(buf_ref):
    @pl.when(pl.program_id(1) == 0)
    def _():
        buf_ref[0:HALO_ROWS, :] = jnp.zeros((HALO_ROWS, buf_ref.shape[1]), buf_ref.dtype)


def _ffn_kernel(h_ref, gain_ref, wup_ref, cw_ref, cb_ref, wdown_ref, fgain_ref, o_ref,
                gbuf_ref, *, tile_rows, d_ff, final_norm):
    _zero_halo_at_sequence_start(gbuf_ref)
    x = h_ref[...]
    u = _rmsnorm(x, gain_ref[...]).astype(jnp.bfloat16)
    gv = jnp.dot(u, wup_ref[...], preferred_element_type=jnp.float32)
    g = _causal_conv_rows(gv[:, :d_ff], gbuf_ref, cw_ref, tile_rows) + cb_ref[...]
    act = (jax.nn.silu(g) * gv[:, d_ff:]).astype(jnp.bfloat16)
    out = x + jnp.dot(act, wdown_ref[...], preferred_element_type=jnp.float32)
    if final_norm:
        out = _rmsnorm(out, fgain_ref[...])
    o_ref[...] = out


def _ffn_layer(h, gain, wup, cw, cb, wdown, fgain, *, tile_rows, final_norm):
    bsz, seq, d = h.shape
    d_ff = wdown.shape[0]
    kernel = functools.partial(_ffn_kernel, tile_rows=tile_rows, d_ff=d_ff, final_norm=final_norm)
    return pl.pallas_call(
        kernel,
        grid=(bsz, seq // tile_rows),
        in_specs=[
            _tile_spec(tile_rows, d),
            _const_spec(gain.shape),
            _const_spec(wup.shape), _const_spec(cw.shape), _const_spec(cb.shape),
            _const_spec(wdown.shape), _const_spec(fgain.shape),
        ],
        out_specs=_tile_spec(tile_rows, d),
        out_shape=jax.ShapeDtypeStruct(h.shape, h.dtype),
        scratch_shapes=[pltpu.VMEM((HALO_ROWS + tile_rows, d_ff), jnp.float32)],
        compiler_params=_compiler_params(),
        name="conv_ffn_final" if final_norm else "conv_ffn",
    )(h, gain, wup, cw, cb, wdown, fgain)


def _shortconv_kernel(h_ref, gain_ref, win_ref, cw_ref, wout_ref, o_ref, qbuf_ref, *, tile_rows):
    _zero_halo_at_sequence_start(qbuf_ref)
    x = h_ref[...]
    d = x.shape[-1]
    u = _rmsnorm(x, gain_ref[...]).astype(jnp.bfloat16)
    proj = jnp.dot(u, win_ref[...], preferred_element_type=jnp.float32)
    b_gate = proj[:, :d]
    q = proj[:, d:2 * d] * proj[:, 2 * d:]
    v = _causal_conv_rows(q, qbuf_ref, cw_ref, tile_rows)
    o_ref[...] = x + jnp.dot((b_gate * v).astype(jnp.bfloat16), wout_ref[...],
                             preferred_element_type=jnp.float32)


def _shortconv_layer(h, gain, win, cw, wout, *, tile_rows):
    bsz, seq, d = h.shape
    kernel = functools.partial(_shortconv_kernel, tile_rows=tile_rows)
    return pl.pallas_call(
        kernel,
        grid=(bsz, seq // tile_rows),
        in_specs=[
            _tile_spec(tile_rows, d),
            _const_spec(gain.shape),
            _const_spec(win.shape), _const_spec(cw.shape), _const_spec(wout.shape),
        ],
        out_specs=_tile_spec(tile_rows, d),
        out_shape=jax.ShapeDtypeStruct(h.shape, h.dtype),
        scratch_shapes=[pltpu.VMEM((HALO_ROWS + tile_rows, d), jnp.float32)],
        compiler_params=_compiler_params(),
        name="shortconv_mixer",
    )(h, gain, win, cw, wout)


def _s5_tables(a_re, a_im, log_dt):
    f32 = jnp.float32
    lam_r = a_re.astype(f32)
    lam_i = a_im.astype(f32)
    dt = jnp.exp(log_dt.astype(f32))[:, None]
    n_modes = lam_r.size
    mid = SCAN_ROWS // 2

    def power(k):
        k = k.astype(f32)[:, None, None]
        mag = jnp.exp(lam_r * dt * k)
        ang = lam_i * dt * k
        return ((mag * jnp.cos(ang)).reshape(-1, n_modes),
                (mag * jnp.sin(ang)).reshape(-1, n_modes))

    rows = jnp.arange(SCAN_ROWS)
    er, ei = power(mid - rows)
    fr, fi = power(rows - mid)
    gr, gi = power(jnp.array([mid + 1]))
    return er, ei, fr, fi, gr, gi


def _s5_weights(a_re, a_im, log_dt, b_re, b_im, c_re, c_im):
    f32 = jnp.float32
    lam_r = a_re.astype(f32)
    lam_i = a_im.astype(f32)
    dt = jnp.exp(log_dt.astype(f32))[:, None]
    mag = jnp.exp(lam_r * dt)
    ab_r = mag * jnp.cos(lam_i * dt)
    ab_i = mag * jnp.sin(lam_i * dt)
    den = lam_r * lam_r + lam_i * lam_i
    nr = ab_r - 1.0
    g_r = ((nr * lam_r + ab_i * lam_i) / den)[..., None]
    g_i = ((ab_i * lam_r - nr * lam_i) / den)[..., None]
    br = b_re.astype(f32)
    bi = b_im.astype(f32)
    bb_r = g_r * br - g_i * bi
    bb_i = g_r * bi + g_i * br
    n_groups, n_state, n_ch = bb_r.shape
    gpb = GROUP_BLOCK_CH // n_ch
    n_blocks = n_groups // gpb
    eye = jnp.eye(gpb, dtype=f32)

    def in_proj(w):
        w = w.reshape(n_blocks, gpb, n_state, n_ch)
        return jnp.einsum('bgph,gk->bghkp', w, eye).reshape(n_blocks, gpb * n_ch, gpb * n_state)

    def out_proj(w):
        w = w.reshape(n_blocks, gpb, n_ch, n_state)
        return jnp.einsum('bghp,gk->bgpkh', w, eye).reshape(n_blocks, gpb * n_state, gpb * n_ch)

    wb = jnp.concatenate([in_proj(bb_r), in_proj(bb_i)], axis=2).astype(jnp.bfloat16)
    wc = jnp.concatenate([out_proj(c_re.astype(f32)), out_proj(-c_im.astype(f32))],
                         axis=1).astype(jnp.bfloat16)
    return wb, wc


def kernel(x, norm_mix, norm_ffn, norm_final, s5_a_re, s5_a_im, s5_log_dt, s5_b_re, s5_b_im,
           s5_c_re, s5_c_im, s5_d, s5_w_glu, sc_w_in, sc_conv_w, sc_w_out,
           ffn_w_up, ffn_conv_w, ffn_conv_b, ffn_w_down):
    bf16 = jnp.bfloat16
    row = lambda v: v.reshape(1, -1)
    tri = jnp.tril(jnp.ones((SCAN_ROWS, SCAN_ROWS), jnp.float32)).astype(bf16)

    tabs = _s5_tables(s5_a_re[0], s5_a_im[0], s5_log_dt[0])
    wb, wc = _s5_weights(s5_a_re[0], s5_a_im[0], s5_log_dt[0], s5_b_re[0], s5_b_im[0],
                         s5_c_re[0], s5_c_im[0])
    h = _s5_layer(x, row(norm_mix[0]), tabs, tri, wb, wc, row(s5_d[0]), s5_w_glu[0].astype(bf16),
                  tile_rows=256)
    h = _ffn_layer(h, row(norm_ffn[0]), ffn_w_up[0].astype(bf16), ffn_conv_w[0],
                   row(ffn_conv_b[0]), ffn_w_down[0].astype(bf16), row(norm_final),
                   tile_rows=256, final_norm=False)
    h = _shortconv_layer(h, row(norm_mix[1]), sc_w_in[0].astype(bf16), sc_conv_w[0],
                         sc_w_out[0].astype(bf16), tile_rows=256)
    h = _ffn_layer(h, row(norm_ffn[1]), ffn_w_up[1].astype(bf16), ffn_conv_w[1],
                   row(ffn_conv_b[1]), ffn_w_down[1].astype(bf16), row(norm_final),
                   tile_rows=256, final_norm=True)
    return h
```

```python
import functools

import jax
import jax.numpy as jnp
from jax import lax
from jax.experimental import pallas as pl
from jax.experimental.pallas import tpu as pltpu

RMS_EPS = 1e-6
S5_GROUP = 16
S5_STATE = 64
CONV_WIDTH = 3
HALO_ROWS = 8

SCAN_ROWS = 128
GROUP_BLOCK_CH = 256
COL_BLOCK = 256
VMEM_LIMIT_BYTES = 56 * 1024 * 1024


def _rmsnorm(x, gain):
    ms = jnp.mean(x * x, axis=-1, keepdims=True)
    return x * lax.rsqrt(ms + RMS_EPS) * gain


def _const_spec(shape):
    nd = len(shape)
    return pl.BlockSpec(shape, lambda b, i: (0,) * nd, pipeline_mode=pl.Buffered(1))


def _tile_spec(tile_rows, d):
    return pl.BlockSpec((None, tile_rows, d), lambda b, i: (b, i, 0))


def _compiler_params():
    return pltpu.CompilerParams(
        dimension_semantics=("arbitrary", "arbitrary"),
        vmem_limit_bytes=VMEM_LIMIT_BYTES,
    )


def _s5_kernel(h_ref, gain_ref, er_ref, ei_ref, fr_ref, fi_ref, gr_ref, gi_ref,
               tri_ref, wb_ref, wc_ref, dskip_ref, wglu_ref, o_ref,
               carry_ref, hstate_ref, y_ref, *, tile_rows, n_blocks, modes):
    @pl.when(pl.program_id(1) == 0)
    def _():
        carry_ref[...] = jnp.zeros_like(carry_ref)

    x = h_ref[...]
    u = _rmsnorm(x, gain_ref[...])
    u_bf = u.astype(jnp.bfloat16)
    tri = tri_ref[...]
    n_chunks = tile_rows // SCAN_ROWS

    for gb in range(n_blocks):
        ch = slice(gb * GROUP_BLOCK_CH, (gb + 1) * GROUP_BLOCK_CH)
        md = slice(gb * modes, (gb + 1) * modes)
        bu = jnp.dot(u_bf[:, ch], wb_ref[gb], preferred_element_type=jnp.float32)
        er, ei = er_ref[:, md], ei_ref[:, md]
        fr, fi = fr_ref[:, md], fi_ref[:, md]
        gr, gi = gr_ref[:, md], gi_ref[:, md]
        for c in range(n_chunks):
            rows = slice(c * SCAN_ROWS, (c + 1) * SCAN_ROWS)
            bur = bu[rows, :modes]
            bui = bu[rows, modes:]
            zr = er * bur - ei * bui
            zi = er * bui + ei * bur
            z = jnp.concatenate([zr, zi], axis=1).astype(jnp.bfloat16)
            xs = jnp.dot(tri, z, preferred_element_type=jnp.float32)
            xr = xs[:, :modes] + carry_ref[0:1, md]
            xi = xs[:, modes:] + carry_ref[1:2, md]
            hr = fr * xr - fi * xi
            hi = fr * xi + fi * xr
            hstate_ref[rows, :modes] = hr.astype(jnp.bfloat16)
            hstate_ref[rows, modes:] = hi.astype(jnp.bfloat16)
            hpr = hr[SCAN_ROWS - 1:SCAN_ROWS, :]
            hpi = hi[SCAN_ROWS - 1:SCAN_ROWS, :]
            carry_ref[0:1, md] = gr * hpr - gi * hpi
            carry_ref[1:2, md] = gr * hpi + gi * hpr
        y_ref[:, ch] = jnp.dot(hstate_ref[...], wc_ref[gb], preferred_element_type=jnp.float32)

    y = y_ref[...] + dskip_ref[...] * u
    z = jax.nn.gelu(y).astype(jnp.bfloat16)
    d = x.shape[-1]
    for c in range(d // COL_BLOCK):
        cols = slice(c * COL_BLOCK, (c + 1) * COL_BLOCK)
        gcols = slice(d + c * COL_BLOCK, d + (c + 1) * COL_BLOCK)
        za = jnp.dot(z, wglu_ref[:, cols], preferred_element_type=jnp.float32)
        zg = jnp.dot(z, wglu_ref[:, gcols], preferred_element_type=jnp.float32)
        o_ref[:, cols] = x[:, cols] + za * jax.nn.sigmoid(zg)


def _s5_layer(h, gain, tabs, tri, wb, wc, dskip, wglu, *, tile_rows):
    bsz, seq, d = h.shape
    n_blocks = d // GROUP_BLOCK_CH
    modes = (GROUP_BLOCK_CH // S5_GROUP) * S5_STATE
    er, ei, fr, fi, gr, gi = tabs
    kernel = functools.partial(_s5_kernel, tile_rows=tile_rows, n_blocks=n_blocks, modes=modes)
    return pl.pallas_call(
        kernel,
        grid=(bsz, seq // tile_rows),
        in_specs=[
            _tile_spec(tile_rows, d),
            _const_spec(gain.shape),
            _const_spec(er.shape), _const_spec(ei.shape),
            _const_spec(fr.shape), _const_spec(fi.shape),
            _const_spec(gr.shape), _const_spec(gi.shape),
            _const_spec(tri.shape),
            _const_spec(wb.shape), _const_spec(wc.shape),
            _const_spec(dskip.shape), _const_spec(wglu.shape),
        ],
        out_specs=_tile_spec(tile_rows, d),
        out_shape=jax.ShapeDtypeStruct(h.shape, h.dtype),
        scratch_shapes=[
            pltpu.VMEM((HALO_ROWS, n_blocks * modes), jnp.float32),
            pltpu.VMEM((tile_rows, 2 * modes), jnp.bfloat16),
            pltpu.VMEM((tile_rows, d), jnp.float32),
        ],
        compiler_params=_compiler_params(),
        name="s5_mixer",
    )(h, gain, er, ei, fr, fi, gr, gi, tri, wb, wc, dskip, wglu)


def _causal_conv_rows(val, halo_ref, w_ref):
    tile_rows = val.shape[0]
    ext = jnp.concatenate([halo_ref[...], val], axis=0)
    out = val * w_ref[CONV_WIDTH - 1:CONV_WIDTH, :]
    for k in range(CONV_WIDTH - 1):
        shift = CONV_WIDTH - 1 - k
        out = out + ext[HALO_ROWS - shift:HALO_ROWS - shift + tile_rows, :] * w_ref[k:k + 1, :]
    halo_ref[...] = val[tile_rows - HALO_ROWS:, :]
    return out


def _zero_halo_at_sequence_start(halo_ref):
    @pl.when(pl.program_id(1) == 0)
    def _():
        halo_ref[...] = jnp.zeros_like(halo_ref)


def _ffn_kernel(h_ref, gain_ref, wup_ref, cw_ref, cb_ref, wdown_ref, fgain_ref, o_ref,
                halo_ref, *, d_ff, final_norm):
    _zero_halo_at_sequence_start(halo_ref)
    x = h_ref[...]
    u = _rmsnorm(x, gain_ref[...]).astype(jnp.bfloat16)
    g = jnp.dot(u, wup_ref[:, :d_ff], preferred_element_type=jnp.float32)
    v = jnp.dot(u, wup_ref[:, d_ff:], preferred_element_type=jnp.float32)
    g = _causal_conv_rows(g, halo_ref, cw_ref) + cb_ref[...]
    act = (jax.nn.silu(g) * v).astype(jnp.bfloat16)
    out = x + jnp.dot(act, wdown_ref[...], preferred_element_type=jnp.float32)
    if final_norm:
        out = _rmsnorm(out, fgain_ref[...])
    o_ref[...] = out


def _ffn_layer(h, gain, wup, cw, cb, wdown, fgain, *, tile_rows, final_norm):
    bsz, seq, d = h.shape
    d_ff = wdown.shape[0]
    kernel = functools.partial(_ffn_kernel, d_ff=d_ff, final_norm=final_norm)
    return pl.pallas_call(
        kernel,
        grid=(bsz, seq // tile_rows),
        in_specs=[
            _tile_spec(tile_rows, d),
            _const_spec(gain.shape),
            _const_spec(wup.shape), _const_spec(cw.shape), _const_spec(cb.shape),
            _const_spec(wdown.shape), _const_spec(fgain.shape),
        ],
        out_specs=_tile_spec(tile_rows, d),
        out_shape=jax.ShapeDtypeStruct(h.shape, h.dtype),
        scratch_shapes=[pltpu.VMEM((HALO_ROWS, d_ff), jnp.float32)],
        compiler_params=_compiler_params(),
        name="conv_ffn_final" if final_norm else "conv_ffn",
    )(h, gain, wup, cw, cb, wdown, fgain)


def _shortconv_kernel(h_ref, gain_ref, win_ref, cw_ref, wout_ref, o_ref, halo_ref):
    _zero_halo_at_sequence_start(halo_ref)
    x = h_ref[...]
    d = x.shape[-1]
    u = _rmsnorm(x, gain_ref[...]).astype(jnp.bfloat16)
    b_gate, c_gate, hh = [
        jnp.dot(u, win_ref[:, k * d:(k + 1) * d], preferred_element_type=jnp.float32)
        for k in range(3)]
    v = _causal_conv_rows(c_gate * hh, halo_ref, cw_ref)
    o_ref[...] = x + jnp.dot((b_gate * v).astype(jnp.bfloat16), wout_ref[...],
                             preferred_element_type=jnp.float32)


def _shortconv_layer(h, gain, win, cw, wout, *, tile_rows):
    bsz, seq, d = h.shape
    return pl.pallas_call(
        _shortconv_kernel,
        grid=(bsz, seq // tile_rows),
        in_specs=[
            _tile_spec(tile_rows, d),
            _const_spec(gain.shape),
            _const_spec(win.shape), _const_spec(cw.shape), _const_spec(wout.shape),
        ],
        out_specs=_tile_spec(tile_rows, d),
        out_shape=jax.ShapeDtypeStruct(h.shape, h.dtype),
        scratch_shapes=[pltpu.VMEM((HALO_ROWS, d), jnp.float32)],
        compiler_params=_compiler_params(),
        name="shortconv_mixer",
    )(h, gain, win, cw, wout)


def _s5_tables(a_re, a_im, log_dt):
    f32 = jnp.float32
    lam_r = a_re.astype(f32)
    lam_i = a_im.astype(f32)
    dt = jnp.exp(log_dt.astype(f32))[:, None]
    n_modes = lam_r.size
    mid = SCAN_ROWS // 2

    def power(k):
        k = k.astype(f32)[:, None, None]
        mag = jnp.exp(lam_r * dt * k)
        ang = lam_i * dt * k
        return ((mag * jnp.cos(ang)).reshape(-1, n_modes),
                (mag * jnp.sin(ang)).reshape(-1, n_modes))

    rows = jnp.arange(SCAN_ROWS)
    er, ei = power(mid - rows)
    fr, fi = power(rows - mid)
    gr, gi = power(jnp.array([mid + 1]))
    return er, ei, fr, fi, gr, gi


def _s5_weights(a_re, a_im, log_dt, b_re, b_im, c_re, c_im):
    f32 = jnp.float32
    lam_r = a_re.astype(f32)
    lam_i = a_im.astype(f32)
    dt = jnp.exp(log_dt.astype(f32))[:, None]
    mag = jnp.exp(lam_r * dt)
    ab_r = mag * jnp.cos(lam_i * dt)
    ab_i = mag * jnp.sin(lam_i * dt)
    den = lam_r * lam_r + lam_i * lam_i
    nr = ab_r - 1.0
    g_r = ((nr * lam_r + ab_i * lam_i) / den)[..., None]
    g_i = ((ab_i * lam_r - nr * lam_i) / den)[..., None]
    br = b_re.astype(f32)
    bi = b_im.astype(f32)
    bb_r = g_r * br - g_i * bi
    bb_i = g_r * bi + g_i * br
    n_groups, n_state, n_ch = bb_r.shape
    gpb = GROUP_BLOCK_CH // n_ch
    n_blocks = n_groups // gpb
    eye = jnp.eye(gpb, dtype=f32)

    def in_proj(w):
        w = w.reshape(n_blocks, gpb, n_state, n_ch)
        return jnp.einsum('bgph,gk->bghkp', w, eye).reshape(n_blocks, gpb * n_ch, gpb * n_state)

    def out_proj(w):
        w = w.reshape(n_blocks, gpb, n_ch, n_state)
        return jnp.einsum('bghp,gk->bgpkh', w, eye).reshape(n_blocks, gpb * n_state, gpb * n_ch)

    wb = jnp.concatenate([in_proj(bb_r), in_proj(bb_i)], axis=2).astype(jnp.bfloat16)
    wc = jnp.concatenate([out_proj(c_re.astype(f32)), out_proj(-c_im.astype(f32))],
                         axis=1).astype(jnp.bfloat16)
    return wb, wc


def kernel(x, norm_mix, norm_ffn, norm_final, s5_a_re, s5_a_im, s5_log_dt, s5_b_re, s5_b_im,
           s5_c_re, s5_c_im, s5_d, s5_w_glu, sc_w_in, sc_conv_w, sc_w_out,
           ffn_w_up, ffn_conv_w, ffn_conv_b, ffn_w_down):
    bf16 = jnp.bfloat16
    row = lambda v: v.reshape(1, -1)
    tri = jnp.tril(jnp.ones((SCAN_ROWS, SCAN_ROWS), jnp.float32)).astype(bf16)

    tabs = _s5_tables(s5_a_re[0], s5_a_im[0], s5_log_dt[0])
    wb, wc = _s5_weights(s5_a_re[0], s5_a_im[0], s5_log_dt[0], s5_b_re[0], s5_b_im[0],
                         s5_c_re[0], s5_c_im[0])
    h = _s5_layer(x, row(norm_mix[0]), tabs, tri, wb, wc, row(s5_d[0]), s5_w_glu[0].astype(bf16),
                  tile_rows=512)
    h = _ffn_layer(h, row(norm_ffn[0]), ffn_w_up[0].astype(bf16), ffn_conv_w[0],
                   row(ffn_conv_b[0]), ffn_w_down[0].astype(bf16), row(norm_final),
                   tile_rows=512, final_norm=False)
    h = _shortconv_layer(h, row(norm_mix[1]), sc_w_in[0].astype(bf16), sc_conv_w[0],
                         sc_w_out[0].astype(bf16), tile_rows=512)
    h = _ffn_layer(h, row(norm_ffn[1]), ffn_w_up[1].astype(bf16), ffn_conv_w[1],
                   row(ffn_conv_b[1]), ffn_w_down[1].astype(bf16), row(norm_final),
                   tile_rows=512, final_norm=True)
    return h
```

```python
import functools

import jax
import jax.numpy as jnp
from jax import lax
from jax.experimental import pallas as pl
from jax.experimental.pallas import tpu as pltpu

RMS_EPS = 1e-6
S5_GROUP = 16
S5_STATE = 64
CONV_WIDTH = 3
HALO_ROWS = 8

SCAN_ROWS = 128
GROUP_BLOCK_CH = 256
COL_BLOCK = 256
VMEM_LIMIT_BYTES = 56 * 1024 * 1024


def _rmsnorm(x, gain):
    ms = jnp.mean(x * x, axis=-1, keepdims=True)
    return x * lax.rsqrt(ms + RMS_EPS) * gain


def _const_spec(shape):
    nd = len(shape)
    return pl.BlockSpec(shape, lambda b, i: (0,) * nd, pipeline_mode=pl.Buffered(1))


def _tile_spec(tile_rows, d):
    return pl.BlockSpec((None, tile_rows, d), lambda b, i: (b, i, 0))


def _compiler_params():
    return pltpu.CompilerParams(
        dimension_semantics=("arbitrary", "arbitrary"),
        vmem_limit_bytes=VMEM_LIMIT_BYTES,
    )


def _s5_kernel(h_ref, gain_ref, er_ref, ei_ref, fr_ref, fi_ref, pr_ref, pi_ref,
               tri_ref, wb_ref, wc_ref, dskip_ref, wglu_ref, o_ref,
               carry_ref, hstate_ref, y_ref, *, tile_rows, n_blocks, modes):
    @pl.when(pl.program_id(1) == 0)
    def _():
        carry_ref[...] = jnp.zeros_like(carry_ref)

    bf16 = jnp.bfloat16
    x = h_ref[...]
    u = _rmsnorm(x, gain_ref[...])
    u_bf = u.astype(bf16)
    tri = tri_ref[...]
    n_chunks = tile_rows // SCAN_ROWS

    for gb in range(n_blocks):
        ch = slice(gb * GROUP_BLOCK_CH, (gb + 1) * GROUP_BLOCK_CH)
        md = slice(gb * modes, (gb + 1) * modes)
        bu_r = jnp.dot(u_bf[:, ch], wb_ref[gb, :, :modes], preferred_element_type=jnp.float32)
        bu_i = jnp.dot(u_bf[:, ch], wb_ref[gb, :, modes:], preferred_element_type=jnp.float32)
        er, ei = er_ref[:, md], ei_ref[:, md]
        fr, fi = fr_ref[:, md], fi_ref[:, md]
        pr, pi = pr_ref[:, md], pi_ref[:, md]
        for c in range(n_chunks):
            rows = slice(c * SCAN_ROWS, (c + 1) * SCAN_ROWS)
            bur = bu_r[rows, :].astype(bf16)
            bui = bu_i[rows, :].astype(bf16)
            xs_r = jnp.dot(tri, er * bur - ei * bui, preferred_element_type=jnp.float32)
            xs_i = jnp.dot(tri, er * bui + ei * bur, preferred_element_type=jnp.float32)
            c_r, c_i = carry_ref[0:1, md], carry_ref[1:2, md]
            xr = (xs_r + c_r).astype(bf16)
            xi = (xs_i + c_i).astype(bf16)
            hstate_ref[rows, :modes] = fr * xr - fi * xi
            hstate_ref[rows, modes:] = fr * xi + fi * xr
            xlr = xs_r[SCAN_ROWS - 1:SCAN_ROWS, :] + c_r
            xli = xs_i[SCAN_ROWS - 1:SCAN_ROWS, :] + c_i
            carry_ref[0:1, md] = pr * xlr - pi * xli
            carry_ref[1:2, md] = pr * xli + pi * xlr
        y_ref[:, ch] = jnp.dot(hstate_ref[...], wc_ref[gb], preferred_element_type=jnp.float32)

    y = y_ref[...] + dskip_ref[...] * u
    z = jax.nn.gelu(y).astype(jnp.bfloat16)
    d = x.shape[-1]
    for c in range(d // COL_BLOCK):
        cols = slice(c * COL_BLOCK, (c + 1) * COL_BLOCK)
        gcols = slice(d + c * COL_BLOCK, d + (c + 1) * COL_BLOCK)
        za = jnp.dot(z, wglu_ref[:, cols], preferred_element_type=jnp.float32)
        zg = jnp.dot(z, wglu_ref[:, gcols], preferred_element_type=jnp.float32)
        o_ref[:, cols] = x[:, cols] + za * jax.nn.sigmoid(zg)


def _s5_layer(h, gain, tabs, tri, wb, wc, dskip, wglu, *, tile_rows):
    bsz, seq, d = h.shape
    n_blocks = d // GROUP_BLOCK_CH
    modes = (GROUP_BLOCK_CH // S5_GROUP) * S5_STATE
    er, ei, fr, fi, pr, pi = tabs
    kernel = functools.partial(_s5_kernel, tile_rows=tile_rows, n_blocks=n_blocks, modes=modes)
    return pl.pallas_call(
        kernel,
        grid=(bsz, seq // tile_rows),
        in_specs=[
            _tile_spec(tile_rows, d),
            _const_spec(gain.shape),
            _const_spec(er.shape), _const_spec(ei.shape),
            _const_spec(fr.shape), _const_spec(fi.shape),
            _const_spec(pr.shape), _const_spec(pi.shape),
            _const_spec(tri.shape),
            _const_spec(wb.shape), _const_spec(wc.shape),
            _const_spec(dskip.shape), _const_spec(wglu.shape),
        ],
        out_specs=_tile_spec(tile_rows, d),
        out_shape=jax.ShapeDtypeStruct(h.shape, h.dtype),
        scratch_shapes=[
            pltpu.VMEM((HALO_ROWS, n_blocks * modes), jnp.float32),
            pltpu.VMEM((tile_rows, 2 * modes), jnp.bfloat16),
            pltpu.VMEM((tile_rows, d), jnp.float32),
        ],
        compiler_params=_compiler_params(),
        name="s5_mixer",
    )(h, gain, er, ei, fr, fi, pr, pi, tri, wb, wc, dskip, wglu)


def _causal_conv_rows(val, halo_ref, w_ref):
    tile_rows = val.shape[0]
    ext = jnp.concatenate([halo_ref[...], val], axis=0)
    out = val * w_ref[CONV_WIDTH - 1:CONV_WIDTH, :]
    for k in range(CONV_WIDTH - 1):
        shift = CONV_WIDTH - 1 - k
        out = out + ext[HALO_ROWS - shift:HALO_ROWS - shift + tile_rows, :] * w_ref[k:k + 1, :]
    halo_ref[...] = val[tile_rows - HALO_ROWS:, :]
    return out


def _zero_halo_at_sequence_start(halo_ref):
    @pl.when(pl.program_id(1) == 0)
    def _():
        halo_ref[...] = jnp.zeros_like(halo_ref)


def _ffn_kernel(h_ref, gain_ref, wup_ref, cw_ref, cb_ref, wdown_ref, fgain_ref, o_ref,
                halo_ref, *, d_ff, final_norm):
    _zero_halo_at_sequence_start(halo_ref)
    x = h_ref[...]
    u = _rmsnorm(x, gain_ref[...]).astype(jnp.bfloat16)
    g = jnp.dot(u, wup_ref[:, :d_ff], preferred_element_type=jnp.float32)
    v = jnp.dot(u, wup_ref[:, d_ff:], preferred_element_type=jnp.float32)
    g = _causal_conv_rows(g, halo_ref, cw_ref) + cb_ref[...]
    act = (jax.nn.silu(g) * v).astype(jnp.bfloat16)
    out = x + jnp.dot(act, wdown_ref[...], preferred_element_type=jnp.float32)
    if final_norm:
        out = _rmsnorm(out, fgain_ref[...])
    o_ref[...] = out


def _ffn_layer(h, gain, wup, cw, cb, wdown, fgain, *, tile_rows, final_norm):
    bsz, seq, d = h.shape
    d_ff = wdown.shape[0]
    kernel = functools.partial(_ffn_kernel, d_ff=d_ff, final_norm=final_norm)
    return pl.pallas_call(
        kernel,
        grid=(bsz, seq // tile_rows),
        in_specs=[
            _tile_spec(tile_rows, d),
            _const_spec(gain.shape),
            _const_spec(wup.shape), _const_spec(cw.shape), _const_spec(cb.shape),
            _const_spec(wdown.shape), _const_spec(fgain.shape),
        ],
        out_specs=_tile_spec(tile_rows, d),
        out_shape=jax.ShapeDtypeStruct(h.shape, h.dtype),
        scratch_shapes=[pltpu.VMEM((HALO_ROWS, d_ff), jnp.float32)],
        compiler_params=_compiler_params(),
        name="conv_ffn_final" if final_norm else "conv_ffn",
    )(h, gain, wup, cw, cb, wdown, fgain)


def _shortconv_kernel(h_ref, gain_ref, win_ref, cw_ref, wout_ref, o_ref, halo_ref):
    _zero_halo_at_sequence_start(halo_ref)
    x = h_ref[...]
    d = x.shape[-1]
    u = _rmsnorm(x, gain_ref[...]).astype(jnp.bfloat16)
    b_gate, c_gate, hh = [
        jnp.dot(u, win_ref[:, k * d:(k + 1) * d], preferred_element_type=jnp.float32)
        for k in range(3)]
    v = _causal_conv_rows(c_gate * hh, halo_ref, cw_ref)
    o_ref[...] = x + jnp.dot((b_gate * v).astype(jnp.bfloat16), wout_ref[...],
                             preferred_element_type=jnp.float32)


def _shortconv_layer(h, gain, win, cw, wout, *, tile_rows):
    bsz, seq, d = h.shape
    return pl.pallas_call(
        _shortconv_kernel,
        grid=(bsz, seq // tile_rows),
        in_specs=[
            _tile_spec(tile_rows, d),
            _const_spec(gain.shape),
            _const_spec(win.shape), _const_spec(cw.shape), _const_spec(wout.shape),
        ],
        out_specs=_tile_spec(tile_rows, d),
        out_shape=jax.ShapeDtypeStruct(h.shape, h.dtype),
        scratch_shapes=[pltpu.VMEM((HALO_ROWS, d), jnp.float32)],
        compiler_params=_compiler_params(),
        name="shortconv_mixer",
    )(h, gain, win, cw, wout)


def _s5_tables(a_re, a_im, log_dt):
    f32 = jnp.float32
    lam_r = a_re.astype(f32)
    lam_i = a_im.astype(f32)
    dt = jnp.exp(log_dt.astype(f32))[:, None]
    n_modes = lam_r.size
    mid = SCAN_ROWS // 2

    def power(k):
        k = k.astype(f32)[:, None, None]
        mag = jnp.exp(lam_r * dt * k)
        ang = lam_i * dt * k
        return ((mag * jnp.cos(ang)).reshape(-1, n_modes),
                (mag * jnp.sin(ang)).reshape(-1, n_modes))

    rows = jnp.arange(SCAN_ROWS)
    er, ei = power(mid - rows)
    fr, fi = power(rows - mid)
    pr, pi = power(jnp.array([SCAN_ROWS]))
    bf16 = jnp.bfloat16
    return er.astype(bf16), ei.astype(bf16), fr.astype(bf16), fi.astype(bf16), pr, pi


def _s5_weights(a_re, a_im, log_dt, b_re, b_im, c_re, c_im):
    f32 = jnp.float32
    lam_r = a_re.astype(f32)
    lam_i = a_im.astype(f32)
    dt = jnp.exp(log_dt.astype(f32))[:, None]
    mag = jnp.exp(lam_r * dt)
    ab_r = mag * jnp.cos(lam_i * dt)
    ab_i = mag * jnp.sin(lam_i * dt)
    den = lam_r * lam_r + lam_i * lam_i
    nr = ab_r - 1.0
    g_r = ((nr * lam_r + ab_i * lam_i) / den)[..., None]
    g_i = ((ab_i * lam_r - nr * lam_i) / den)[..., None]
    br = b_re.astype(f32)
    bi = b_im.astype(f32)
    bb_r = g_r * br - g_i * bi
    bb_i = g_r * bi + g_i * br
    n_groups, n_state, n_ch = bb_r.shape
    gpb = GROUP_BLOCK_CH // n_ch
    n_blocks = n_groups // gpb
    eye = jnp.eye(gpb, dtype=f32)

    def in_proj(w):
        w = w.reshape(n_blocks, gpb, n_state, n_ch)
        return jnp.einsum('bgph,gk->bghkp', w, eye).reshape(n_blocks, gpb * n_ch, gpb * n_state)

    def out_proj(w):
        w = w.reshape(n_blocks, gpb, n_ch, n_state)
        return jnp.einsum('bghp,gk->bgpkh', w, eye).reshape(n_blocks, gpb * n_state, gpb * n_ch)

    wb = jnp.concatenate([in_proj(bb_r), in_proj(bb_i)], axis=2).astype(jnp.bfloat16)
    wc = jnp.concatenate([out_proj(c_re.astype(f32)), out_proj(-c_im.astype(f32))],
                         axis=1).astype(jnp.bfloat16)
    return wb, wc


def kernel(x, norm_mix, norm_ffn, norm_final, s5_a_re, s5_a_im, s5_log_dt, s5_b_re, s5_b_im,
           s5_c_re, s5_c_im, s5_d, s5_w_glu, sc_w_in, sc_conv_w, sc_w_out,
           ffn_w_up, ffn_conv_w, ffn_conv_b, ffn_w_down):
    bf16 = jnp.bfloat16
    row = lambda v: v.reshape(1, -1)
    tri = jnp.tril(jnp.ones((SCAN_ROWS, SCAN_ROWS), jnp.float32)).astype(bf16)

    tabs = _s5_tables(s5_a_re[0], s5_a_im[0], s5_log_dt[0])
    wb, wc = _s5_weights(s5_a_re[0], s5_a_im[0], s5_log_dt[0], s5_b_re[0], s5_b_im[0],
                         s5_c_re[0], s5_c_im[0])
    h = _s5_layer(x, row(norm_mix[0]), tabs, tri, wb, wc, row(s5_d[0]), s5_w_glu[0].astype(bf16),
                  tile_rows=512)
    h = _ffn_layer(h, row(norm_ffn[0]), ffn_w_up[0].astype(bf16), ffn_conv_w[0],
                   row(ffn_conv_b[0]), ffn_w_down[0].astype(bf16), row(norm_final),
                   tile_rows=512, final_norm=False)
    h = _shortconv_layer(h, row(norm_mix[1]), sc_w_in[0].astype(bf16), sc_conv_w[0],
                         sc_w_out[0].astype(bf16), tile_rows=512)
    h = _ffn_layer(h, row(norm_ffn[1]), ffn_w_up[1].astype(bf16), ffn_conv_w[1],
                   row(ffn_conv_b[1]), ffn_w_down[1].astype(bf16), row(norm_final),
                   tile_rows=512, final_norm=True)
    return h
```

```python
import functools

import jax
import jax.numpy as jnp
from jax import lax
from jax.experimental import pallas as pl
from jax.experimental.pallas import tpu as pltpu

RMS_EPS = 1e-6
S5_GROUP = 16
S5_STATE = 64
CONV_WIDTH = 3
HALO_ROWS = 8

SCAN_ROWS = 128
GROUP_BLOCK_CH = 256
COL_BLOCK = 256
ROW_BLOCK = 512
VMEM_LIMIT_BYTES = 56 * 1024 * 1024


def _rmsnorm(x, gain):
    ms = jnp.mean(x * x, axis=-1, keepdims=True)
    return x * lax.rsqrt(ms + RMS_EPS) * gain


def _const_spec(shape):
    nd = len(shape)
    return pl.BlockSpec(shape, lambda b, i: (0,) * nd, pipeline_mode=pl.Buffered(1))


def _tile_spec(tile_rows, d):
    return pl.BlockSpec((None, tile_rows, d), lambda b, i: (b, i, 0))


def _compiler_params():
    return pltpu.CompilerParams(
        dimension_semantics=("arbitrary", "arbitrary"),
        vmem_limit_bytes=VMEM_LIMIT_BYTES,
    )


def _s5_kernel(h_ref, gain_ref, er_ref, ei_ref, fr_ref, fi_ref, pr_ref, pi_ref,
               tri_ref, wb_ref, wc_ref, dskip_ref, wglu_ref, o_ref,
               carry_ref, hstate_ref, y_ref, *, tile_rows, n_blocks, modes):
    @pl.when(pl.program_id(1) == 0)
    def _():
        carry_ref[...] = jnp.zeros_like(carry_ref)

    bf16 = jnp.bfloat16
    x = h_ref[...]
    u = _rmsnorm(x, gain_ref[...])
    u_bf = u.astype(bf16)
    tri = tri_ref[...]
    n_chunks = tile_rows // SCAN_ROWS

    for gb in range(n_blocks):
        ch = slice(gb * GROUP_BLOCK_CH, (gb + 1) * GROUP_BLOCK_CH)
        md = slice(gb * modes, (gb + 1) * modes)
        bu_r = jnp.dot(u_bf[:, ch], wb_ref[gb, :, :modes], preferred_element_type=jnp.float32)
        bu_i = jnp.dot(u_bf[:, ch], wb_ref[gb, :, modes:], preferred_element_type=jnp.float32)
        er, ei = er_ref[:, md], ei_ref[:, md]
        fr, fi = fr_ref[:, md], fi_ref[:, md]
        pr, pi = pr_ref[:, md], pi_ref[:, md]
        for c in range(n_chunks):
            rows = slice(c * SCAN_ROWS, (c + 1) * SCAN_ROWS)
            bur = bu_r[rows, :].astype(bf16)
            bui = bu_i[rows, :].astype(bf16)
            xs_r = jnp.dot(tri, er * bur - ei * bui, preferred_element_type=jnp.float32)
            xs_i = jnp.dot(tri, er * bui + ei * bur, preferred_element_type=jnp.float32)
            c_r, c_i = carry_ref[0:1, md], carry_ref[1:2, md]
            xr = (xs_r + c_r).astype(bf16)
            xi = (xs_i + c_i).astype(bf16)
            hstate_ref[rows, :modes] = fr * xr - fi * xi
            hstate_ref[rows, modes:] = fr * xi + fi * xr
            xlr = xs_r[SCAN_ROWS - 1:SCAN_ROWS, :] + c_r
            xli = xs_i[SCAN_ROWS - 1:SCAN_ROWS, :] + c_i
            carry_ref[0:1, md] = pr * xlr - pi * xli
            carry_ref[1:2, md] = pr * xli + pi * xlr
        y_ref[:, ch] = jnp.dot(hstate_ref[...], wc_ref[gb], preferred_element_type=jnp.float32)

    y = y_ref[...] + dskip_ref[...] * u
    z = jax.nn.gelu(y).astype(jnp.bfloat16)
    d = x.shape[-1]
    for c in range(d // COL_BLOCK):
        cols = slice(c * COL_BLOCK, (c + 1) * COL_BLOCK)
        gcols = slice(d + c * COL_BLOCK, d + (c + 1) * COL_BLOCK)
        za = jnp.dot(z, wglu_ref[:, cols], preferred_element_type=jnp.float32)
        zg = jnp.dot(z, wglu_ref[:, gcols], preferred_element_type=jnp.float32)
        o_ref[:, cols] = x[:, cols] + za * jax.nn.sigmoid(zg)


def _s5_layer(h, gain, tabs, tri, wb, wc, dskip, wglu, *, tile_rows):
    bsz, seq, d = h.shape
    n_blocks = d // GROUP_BLOCK_CH
    modes = (GROUP_BLOCK_CH // S5_GROUP) * S5_STATE
    er, ei, fr, fi, pr, pi = tabs
    kernel = functools.partial(_s5_kernel, tile_rows=tile_rows, n_blocks=n_blocks, modes=modes)
    return pl.pallas_call(
        kernel,
        grid=(bsz, seq // tile_rows),
        in_specs=[
            _tile_spec(tile_rows, d),
            _const_spec(gain.shape),
            _const_spec(er.shape), _const_spec(ei.shape),
            _const_spec(fr.shape), _const_spec(fi.shape),
            _const_spec(pr.shape), _const_spec(pi.shape),
            _const_spec(tri.shape),
            _const_spec(wb.shape), _const_spec(wc.shape),
            _const_spec(dskip.shape), _const_spec(wglu.shape),
        ],
        out_specs=_tile_spec(tile_rows, d),
        out_shape=jax.ShapeDtypeStruct(h.shape, h.dtype),
        scratch_shapes=[
            pltpu.VMEM((HALO_ROWS, n_blocks * modes), jnp.float32),
            pltpu.VMEM((tile_rows, 2 * modes), jnp.bfloat16),
            pltpu.VMEM((tile_rows, d), jnp.float32),
        ],
        compiler_params=_compiler_params(),
        name="s5_mixer",
    )(h, gain, er, ei, fr, fi, pr, pi, tri, wb, wc, dskip, wglu)


def _causal_conv_rows(val, halo_ref, w_ref):
    tile_rows = val.shape[0]
    ext = jnp.concatenate([halo_ref[...], val], axis=0)
    out = val * w_ref[CONV_WIDTH - 1:CONV_WIDTH, :]
    for k in range(CONV_WIDTH - 1):
        shift = CONV_WIDTH - 1 - k
        out = out + ext[HALO_ROWS - shift:HALO_ROWS - shift + tile_rows, :] * w_ref[k:k + 1, :]
    halo_ref[...] = val[tile_rows - HALO_ROWS:, :]
    return out


def _zero_halo_at_sequence_start(halo_ref):
    @pl.when(pl.program_id(1) == 0)
    def _():
        halo_ref[...] = jnp.zeros_like(halo_ref)


def _ffn_kernel(h_ref, gain_ref, wup_ref, cw_ref, cb_ref, wdown_ref, fgain_ref, o_ref,
                halo_ref, *, d_ff, final_norm):
    _zero_halo_at_sequence_start(halo_ref)
    for r in range(h_ref.shape[0] // ROW_BLOCK):
        rows = slice(r * ROW_BLOCK, (r + 1) * ROW_BLOCK)
        x = h_ref[rows, :]
        u = _rmsnorm(x, gain_ref[...]).astype(jnp.bfloat16)
        g = jnp.dot(u, wup_ref[:, :d_ff], preferred_element_type=jnp.float32)
        v = jnp.dot(u, wup_ref[:, d_ff:], preferred_element_type=jnp.float32)
        g = _causal_conv_rows(g, halo_ref, cw_ref) + cb_ref[...]
        act = (jax.nn.silu(g) * v).astype(jnp.bfloat16)
        out = x + jnp.dot(act, wdown_ref[...], preferred_element_type=jnp.float32)
        if final_norm:
            out = _rmsnorm(out, fgain_ref[...])
        o_ref[rows, :] = out


def _ffn_layer(h, gain, wup, cw, cb, wdown, fgain, *, tile_rows, final_norm):
    bsz, seq, d = h.shape
    d_ff = wdown.shape[0]
    kernel = functools.partial(_ffn_kernel, d_ff=d_ff, final_norm=final_norm)
    return pl.pallas_call(
        kernel,
        grid=(bsz, seq // tile_rows),
        in_specs=[
            _tile_spec(tile_rows, d),
            _const_spec(gain.shape),
            _const_spec(wup.shape), _const_spec(cw.shape), _const_spec(cb.shape),
            _const_spec(wdown.shape), _const_spec(fgain.shape),
        ],
        out_specs=_tile_spec(tile_rows, d),
        out_shape=jax.ShapeDtypeStruct(h.shape, h.dtype),
        scratch_shapes=[pltpu.VMEM((HALO_ROWS, d_ff), jnp.float32)],
        compiler_params=_compiler_params(),
        name="conv_ffn_final" if final_norm else "conv_ffn",
    )(h, gain, wup, cw, cb, wdown, fgain)


def _shortconv_kernel(h_ref, gain_ref, win_ref, cw_ref, wout_ref, o_ref, halo_ref):
    _zero_halo_at_sequence_start(halo_ref)
    d = h_ref.shape[-1]
    for r in range(h_ref.shape[0] // ROW_BLOCK):
        rows = slice(r * ROW_BLOCK, (r + 1) * ROW_BLOCK)
        x = h_ref[rows, :]
        u = _rmsnorm(x, gain_ref[...]).astype(jnp.bfloat16)
        b_gate, c_gate, hh = [
            jnp.dot(u, win_ref[:, k * d:(k + 1) * d], preferred_element_type=jnp.float32)
            for k in range(3)]
        v = _causal_conv_rows(c_gate * hh, halo_ref, cw_ref)
        o_ref[rows, :] = x + jnp.dot((b_gate * v).astype(jnp.bfloat16), wout_ref[...],
                                     preferred_element_type=jnp.float32)


def _shortconv_layer(h, gain, win, cw, wout, *, tile_rows):
    bsz, seq, d = h.shape
    return pl.pallas_call(
        _shortconv_kernel,
        grid=(bsz, seq // tile_rows),
        in_specs=[
            _tile_spec(tile_rows, d),
            _const_spec(gain.shape),
            _const_spec(win.shape), _const_spec(cw.shape), _const_spec(wout.shape),
        ],
        out_specs=_tile_spec(tile_rows, d),
        out_shape=jax.ShapeDtypeStruct(h.shape, h.dtype),
        scratch_shapes=[pltpu.VMEM((HALO_ROWS, d), jnp.float32)],
        compiler_params=_compiler_params(),
        name="shortconv_mixer",
    )(h, gain, win, cw, wout)


def _s5_tables(a_re, a_im, log_dt):
    f32 = jnp.float32
    lam_r = a_re.astype(f32)
    lam_i = a_im.astype(f32)
    dt = jnp.exp(log_dt.astype(f32))[:, None]
    n_modes = lam_r.size
    mid = SCAN_ROWS // 2

    def power(k):
        k = k.astype(f32)[:, None, None]
        mag = jnp.exp(lam_r * dt * k)
        ang = lam_i * dt * k
        return ((mag * jnp.cos(ang)).reshape(-1, n_modes),
                (mag * jnp.sin(ang)).reshape(-1, n_modes))

    rows = jnp.arange(SCAN_ROWS)
    er, ei = power(mid - rows)
    fr, fi = power(rows - mid)
    pr, pi = power(jnp.array([SCAN_ROWS]))
    bf16 = jnp.bfloat16
    return er.astype(bf16), ei.astype(bf16), fr.astype(bf16), fi.astype(bf16), pr, pi


def _s5_weights(a_re, a_im, log_dt, b_re, b_im, c_re, c_im):
    f32 = jnp.float32
    lam_r = a_re.astype(f32)
    lam_i = a_im.astype(f32)
    dt = jnp.exp(log_dt.astype(f32))[:, None]
    mag = jnp.exp(lam_r * dt)
    ab_r = mag * jnp.cos(lam_i * dt)
    ab_i = mag * jnp.sin(lam_i * dt)
    den = lam_r * lam_r + lam_i * lam_i
    nr = ab_r - 1.0
    g_r = ((nr * lam_r + ab_i * lam_i) / den)[..., None]
    g_i = ((ab_i * lam_r - nr * lam_i) / den)[..., None]
    br = b_re.astype(f32)
    bi = b_im.astype(f32)
    bb_r = g_r * br - g_i * bi
    bb_i = g_r * bi + g_i * br
    n_groups, n_state, n_ch = bb_r.shape
    gpb = GROUP_BLOCK_CH // n_ch
    n_blocks = n_groups // gpb
    eye = jnp.eye(gpb, dtype=f32)

    def in_proj(w):
        w = w.reshape(n_blocks, gpb, n_state, n_ch)
        return jnp.einsum('bgph,gk->bghkp', w, eye).reshape(n_blocks, gpb * n_ch, gpb * n_state)

    def out_proj(w):
        w = w.reshape(n_blocks, gpb, n_ch, n_state)
        return jnp.einsum('bghp,gk->bgpkh', w, eye).reshape(n_blocks, gpb * n_state, gpb * n_ch)

    wb = jnp.concatenate([in_proj(bb_r), in_proj(bb_i)], axis=2).astype(jnp.bfloat16)
    wc = jnp.concatenate([out_proj(c_re.astype(f32)), out_proj(-c_im.astype(f32))],
                         axis=1).astype(jnp.bfloat16)
    return wb, wc


def kernel(x, norm_mix, norm_ffn, norm_final, s5_a_re, s5_a_im, s5_log_dt, s5_b_re, s5_b_im,
           s5_c_re, s5_c_im, s5_d, s5_w_glu, sc_w_in, sc_conv_w, sc_w_out,
           ffn_w_up, ffn_conv_w, ffn_conv_b, ffn_w_down):
    bf16 = jnp.bfloat16
    row = lambda v: v.reshape(1, -1)
    tri = jnp.tril(jnp.ones((SCAN_ROWS, SCAN_ROWS), jnp.float32)).astype(bf16)

    tabs = _s5_tables(s5_a_re[0], s5_a_im[0], s5_log_dt[0])
    wb, wc = _s5_weights(s5_a_re[0], s5_a_im[0], s5_log_dt[0], s5_b_re[0], s5_b_im[0],
                         s5_c_re[0], s5_c_im[0])
    h = _s5_layer(x, row(norm_mix[0]), tabs, tri, wb, wc, row(s5_d[0]), s5_w_glu[0].astype(bf16),
                  tile_rows=512)
    h = _ffn_layer(h, row(norm_ffn[0]), ffn_w_up[0].astype(bf16), ffn_conv_w[0],
                   row(ffn_conv_b[0]), ffn_w_down[0].astype(bf16), row(norm_final),
                   tile_rows=1024, final_norm=False)
    h = _shortconv_layer(h, row(norm_mix[1]), sc_w_in[0].astype(bf16), sc_conv_w[0],
                         sc_w_out[0].astype(bf16), tile_rows=1024)
    h = _ffn_layer(h, row(norm_ffn[1]), ffn_w_up[1].astype(bf16), ffn_conv_w[1],
                   row(ffn_conv_b[1]), ffn_w_down[1].astype(bf16), row(norm_final),
                   tile_rows=1024, final_norm=True)
    return h
```

```python
import functools

import jax
import jax.numpy as jnp
from jax import lax
from jax.experimental import pallas as pl
from jax.experimental.pallas import tpu as pltpu

RMS_EPS = 1e-6
S5_GROUP = 16
S5_STATE = 64
CONV_WIDTH = 3
HALO_ROWS = 8

LANES = 128
LAG = 8
PAIR_CH = 32
CHUNK_BLOCK = 16
COL_BLOCK = 256
ROW_BLOCK = 512
VMEM_LIMIT_BYTES = 56 * 1024 * 1024


def _rmsnorm(x, gain):
    ms = jnp.mean(x * x, axis=-1, keepdims=True)
    return x * lax.rsqrt(ms + RMS_EPS) * gain


def _const_spec(shape):
    nd = len(shape)
    return pl.BlockSpec(shape, lambda b, i: (0,) * nd, pipeline_mode=pl.Buffered(1))


def _tile_spec(tile_rows, d):
    return pl.BlockSpec((None, tile_rows, d), lambda b, i: (b, i, 0))


def _compiler_params():
    return pltpu.CompilerParams(
        dimension_semantics=("arbitrary", "arbitrary"),
        vmem_limit_bytes=VMEM_LIMIT_BYTES,
    )


def _s5_kernel(h_ref, gain_ref, er_ref, ei_ref, fr_ref, fi_ref, pr_ref, pi_ref,
               tri_ref, ws_ref, wko_ref, dskip_ref, wglu_ref, o_ref,
               carry_ref, us_ref, ys_ref, hp_ref):
    @pl.when(pl.program_id(1) == 0)
    def _():
        carry_ref[...] = jnp.zeros_like(carry_ref)

    bf16 = jnp.bfloat16
    f32 = jnp.float32
    tile_rows, d = h_ref.shape
    n_chunks = tile_rows // LAG
    n_lane_tiles = d // LANES
    pairs_per_lane_tile = LANES // PAIR_CH
    n_pairs = d // PAIR_CH
    pair_modes = (PAIR_CH // S5_GROUP) * S5_STATE

    x = h_ref[...]
    u = _rmsnorm(x, gain_ref[...])
    for j in range(n_lane_tiles):
        us_ref[j] = u[:, j * LANES:(j + 1) * LANES]
    lag_rows = [[us_ref[j, pl.ds(dd, n_chunks, stride=LAG), :] for j in range(n_lane_tiles)]
                for dd in range(LAG)]

    u2, s_re, s_im = [], [], []
    for k in range(n_pairs):
        j, q = divmod(k, pairs_per_lane_tile)
        lanes = slice(q * PAIR_CH, (q + 1) * PAIR_CH)
        u2k = jnp.concatenate([lag_rows[dd][j][:, lanes] for dd in range(LAG)], axis=1).astype(bf16)
        u2.append(u2k)
        sk = jnp.dot(u2k, ws_ref[k], preferred_element_type=f32)
        s_re.append(sk[:, :pair_modes])
        s_im.append(sk[:, pair_modes:])
    s_re = jnp.concatenate(s_re, axis=1)
    s_im = jnp.concatenate(s_im, axis=1)

    tri = tri_ref[...]
    er, ei, fr, fi = er_ref[...], ei_ref[...], fr_ref[...], fi_ref[...]
    pr, pi = pr_ref[...], pi_ref[...]
    for blk in range(n_chunks // CHUNK_BLOCK):
        rows = slice(blk * CHUNK_BLOCK, (blk + 1) * CHUNK_BLOCK)
        sr = s_re[rows, :].astype(bf16)
        si = s_im[rows, :].astype(bf16)
        xr = jnp.dot(tri, er * sr - ei * si, preferred_element_type=f32)
        xi = jnp.dot(tri, er * si + ei * sr, preferred_element_type=f32)
        c_r, c_i = carry_ref[0:1, :], carry_ref[1:2, :]
        gr = (xr[:CHUNK_BLOCK, :] + c_r).astype(bf16)
        gi = (xi[:CHUNK_BLOCK, :] + c_i).astype(bf16)
        hpr = fr * gr - fi * gi
        hpi = fr * gi + fi * gr
        for k in range(n_pairs):
            md = slice(k * pair_modes, (k + 1) * pair_modes)
            hp_ref[rows, 2 * k * pair_modes:(2 * k + 1) * pair_modes] = hpr[:, md]
            hp_ref[rows, (2 * k + 1) * pair_modes:(2 * k + 2) * pair_modes] = hpi[:, md]
        t_r = xr[CHUNK_BLOCK:CHUNK_BLOCK + 1, :] + c_r
        t_i = xi[CHUNK_BLOCK:CHUNK_BLOCK + 1, :] + c_i
        carry_ref[0:1, :] = pr * t_r - pi * t_i
        carry_ref[1:2, :] = pr * t_i + pi * t_r

    y2 = []
    for k in range(n_pairs):
        lhs = jnp.concatenate([u2[k], hp_ref[:, 2 * k * pair_modes:(2 * k + 2) * pair_modes]], axis=1)
        y2.append(jnp.dot(lhs, wko_ref[k], preferred_element_type=f32))
    for dd in range(LAG):
        lanes = slice(dd * PAIR_CH, (dd + 1) * PAIR_CH)
        for j in range(n_lane_tiles):
            ys_ref[j, pl.ds(dd, n_chunks, stride=LAG), :] = jnp.concatenate(
                [y2[j * pairs_per_lane_tile + q][:, lanes] for q in range(pairs_per_lane_tile)], axis=1)
    y = jnp.concatenate([ys_ref[j] for j in range(n_lane_tiles)], axis=1)

    y = y + dskip_ref[...] * u
    z = jax.nn.gelu(y).astype(bf16)
    for c in range(d // COL_BLOCK):
        cols = slice(c * COL_BLOCK, (c + 1) * COL_BLOCK)
        gcols = slice(d + c * COL_BLOCK, d + (c + 1) * COL_BLOCK)
        za = jnp.dot(z, wglu_ref[:, cols], preferred_element_type=f32)
        zg = jnp.dot(z, wglu_ref[:, gcols], preferred_element_type=f32)
        o_ref[:, cols] = x[:, cols] + za * jax.nn.sigmoid(zg)


def _s5_layer(h, gain, tabs, tri, ws, wko, dskip, wglu, *, tile_rows):
    bsz, seq, d = h.shape
    er, ei, fr, fi, pr, pi = tabs
    n_modes = er.shape[1]
    return pl.pallas_call(
        _s5_kernel,
        grid=(bsz, seq // tile_rows),
        in_specs=[
            _tile_spec(tile_rows, d),
            _const_spec(gain.shape),
            _const_spec(er.shape), _const_spec(ei.shape),
            _const_spec(fr.shape), _const_spec(fi.shape),
            _const_spec(pr.shape), _const_spec(pi.shape),
            _const_spec(tri.shape),
            _const_spec(ws.shape), _const_spec(wko.shape),
            _const_spec(dskip.shape), _const_spec(wglu.shape),
        ],
        out_specs=_tile_spec(tile_rows, d),
        out_shape=jax.ShapeDtypeStruct(h.shape, h.dtype),
        scratch_shapes=[
            pltpu.VMEM((HALO_ROWS, n_modes), jnp.float32),
            pltpu.VMEM((d // LANES, tile_rows, LANES), jnp.float32),
            pltpu.VMEM((d // LANES, tile_rows, LANES), jnp.float32),
            pltpu.VMEM((tile_rows // LAG, 2 * n_modes), jnp.bfloat16),
        ],
        compiler_params=_compiler_params(),
        name="s5_mixer",
    )(h, gain, er, ei, fr, fi, pr, pi, tri, ws, wko, dskip, wglu)


def _causal_conv_rows(val, halo_ref, w_ref):
    tile_rows = val.shape[0]
    ext = jnp.concatenate([halo_ref[...], val], axis=0)
    out = val * w_ref[CONV_WIDTH - 1:CONV_WIDTH, :]
    for k in range(CONV_WIDTH - 1):
        shift = CONV_WIDTH - 1 - k
        out = out + ext[HALO_ROWS - shift:HALO_ROWS - shift + tile_rows, :] * w_ref[k:k + 1, :]
    halo_ref[...] = val[tile_rows - HALO_ROWS:, :]
    return out


def _zero_halo_at_sequence_start(halo_ref):
    @pl.when(pl.program_id(1) == 0)
    def _():
        halo_ref[...] = jnp.zeros_like(halo_ref)


def _ffn_kernel(h_ref, gain_ref, wup_ref, cw_ref, cb_ref, wdown_ref, fgain_ref, o_ref,
                halo_ref, *, d_ff, final_norm):
    _zero_halo_at_sequence_start(halo_ref)
    for r in range(h_ref.shape[0] // ROW_BLOCK):
        rows = slice(r * ROW_BLOCK, (r + 1) * ROW_BLOCK)
        x = h_ref[rows, :]
        u = _rmsnorm(x, gain_ref[...]).astype(jnp.bfloat16)
        g = jnp.dot(u, wup_ref[:, :d_ff], preferred_element_type=jnp.float32)
        v = jnp.dot(u, wup_ref[:, d_ff:], preferred_element_type=jnp.float32)
        g = _causal_conv_rows(g, halo_ref, cw_ref) + cb_ref[...]
        act = (jax.nn.silu(g) * v).astype(jnp.bfloat16)
        out = x + jnp.dot(act, wdown_ref[...], preferred_element_type=jnp.float32)
        if final_norm:
            out = _rmsnorm(out, fgain_ref[...])
        o_ref[rows, :] = out


def _ffn_layer(h, gain, wup, cw, cb, wdown, fgain, *, tile_rows, final_norm):
    bsz, seq, d = h.shape
    d_ff = wdown.shape[0]
    kernel = functools.partial(_ffn_kernel, d_ff=d_ff, final_norm=final_norm)
    return pl.pallas_call(
        kernel,
        grid=(bsz, seq // tile_rows),
        in_specs=[
            _tile_spec(tile_rows, d),
            _const_spec(gain.shape),
            _const_spec(wup.shape), _const_spec(cw.shape), _const_spec(cb.shape),
            _const_spec(wdown.shape), _const_spec(fgain.shape),
        ],
        out_specs=_tile_spec(tile_rows, d),
        out_shape=jax.ShapeDtypeStruct(h.shape, h.dtype),
        scratch_shapes=[pltpu.VMEM((HALO_ROWS, d_ff), jnp.float32)],
        compiler_params=_compiler_params(),
        name="conv_ffn_final" if final_norm else "conv_ffn",
    )(h, gain, wup, cw, cb, wdown, fgain)


def _shortconv_kernel(h_ref, gain_ref, win_ref, cw_ref, wout_ref, o_ref, halo_ref):
    _zero_halo_at_sequence_start(halo_ref)
    d = h_ref.shape[-1]
    for r in range(h_ref.shape[0] // ROW_BLOCK):
        rows = slice(r * ROW_BLOCK, (r + 1) * ROW_BLOCK)
        x = h_ref[rows, :]
        u = _rmsnorm(x, gain_ref[...]).astype(jnp.bfloat16)
        b_gate, c_gate, hh = [
            jnp.dot(u, win_ref[:, k * d:(k + 1) * d], preferred_element_type=jnp.float32)
            for k in range(3)]
        v = _causal_conv_rows(c_gate * hh, halo_ref, cw_ref)
        o_ref[rows, :] = x + jnp.dot((b_gate * v).astype(jnp.bfloat16), wout_ref[...],
                                     preferred_element_type=jnp.float32)


def _shortconv_layer(h, gain, win, cw, wout, *, tile_rows):
    bsz, seq, d = h.shape
    return pl.pallas_call(
        _shortconv_kernel,
        grid=(bsz, seq // tile_rows),
        in_specs=[
            _tile_spec(tile_rows, d),
            _const_spec(gain.shape),
            _const_spec(win.shape), _const_spec(cw.shape), _const_spec(wout.shape),
        ],
        out_specs=_tile_spec(tile_rows, d),
        out_shape=jax.ShapeDtypeStruct(h.shape, h.dtype),
        scratch_shapes=[pltpu.VMEM((HALO_ROWS, d), jnp.float32)],
        compiler_params=_compiler_params(),
        name="shortconv_mixer",
    )(h, gain, win, cw, wout)


def _s5_discretise(a_re, a_im, log_dt, b_re, b_im):
    f32 = jnp.float32
    lam_r = a_re.astype(f32)
    lam_i = a_im.astype(f32)
    dt = jnp.exp(log_dt.astype(f32))[:, None]
    mag = jnp.exp(lam_r * dt)
    ab_r = mag * jnp.cos(lam_i * dt)
    ab_i = mag * jnp.sin(lam_i * dt)
    den = lam_r * lam_r + lam_i * lam_i
    nr = ab_r - 1.0
    g_r = ((nr * lam_r + ab_i * lam_i) / den)[..., None]
    g_i = ((ab_i * lam_r - nr * lam_i) / den)[..., None]
    br = b_re.astype(f32)
    bi = b_im.astype(f32)
    return lam_r * dt, lam_i * dt, g_r * br - g_i * bi, g_r * bi + g_i * br


def _a_power(rate_r, rate_i, k):
    k = jnp.asarray(k, jnp.float32)[..., None, None]
    mag = jnp.exp(rate_r * k)
    return mag * jnp.cos(rate_i * k), mag * jnp.sin(rate_i * k)


def _s5_tables(rate_r, rate_i):
    n_modes = rate_r.size
    mid = CHUNK_BLOCK // 2
    c = jnp.arange(CHUNK_BLOCK)
    flat = lambda t: t.reshape(-1, n_modes)
    er, ei = map(flat, _a_power(rate_r, rate_i, LAG * (mid - c)))
    fr, fi = map(flat, _a_power(rate_r, rate_i, LAG * (c - 1 - mid)))
    pr, pi = map(flat, _a_power(rate_r, rate_i, jnp.array([LAG * CHUNK_BLOCK])))
    bf16 = jnp.bfloat16
    return er.astype(bf16), ei.astype(bf16), fr.astype(bf16), fi.astype(bf16), pr, pi


def _s5_weights(rate_r, rate_i, bb_r, bb_i, c_re, c_im):
    f32 = jnp.float32
    n_groups, n_state, n_ch = bb_r.shape
    gpp = PAIR_CH // n_ch
    n_pairs = n_groups // gpp
    eye = jnp.eye(gpp, dtype=f32)
    cr = c_re.astype(f32)
    ci = c_im.astype(f32)
    lags = jnp.arange(LAG)

    ar, ai = _a_power(rate_r, rate_i, LAG - 1 - lags)
    sr = ar[..., None] * bb_r - ai[..., None] * bb_i
    si = ar[..., None] * bb_i + ai[..., None] * bb_r

    def state_in(w):
        w = w.reshape(LAG, n_pairs, gpp, n_state, n_ch)
        return jnp.einsum('dkgph,gf->kdghfp', w, eye).reshape(n_pairs, LAG * PAIR_CH, gpp * n_state)

    ws = jnp.concatenate([state_in(sr), state_in(si)], axis=2)

    kr, ki = _a_power(rate_r, rate_i, lags)
    kb_r = kr[..., None] * bb_r - ki[..., None] * bb_i
    kb_i = kr[..., None] * bb_i + ki[..., None] * bb_r
    taps = jnp.einsum('gop,jgph->jgoh', cr, kb_r) - jnp.einsum('gop,jgph->jgoh', ci, kb_i)
    diff = lags[None, :] - lags[:, None]
    toep = jnp.where((diff >= 0)[..., None, None, None],
                     taps[jnp.clip(diff, 0, LAG - 1)], 0.0)
    toep = toep.reshape(LAG, LAG, n_pairs, gpp, n_ch, n_ch)
    wk = jnp.einsum('edkgoh,gf->keghdfo', toep, eye).reshape(n_pairs, LAG * PAIR_CH, LAG * PAIR_CH)

    qa_r, qa_i = _a_power(rate_r, rate_i, lags + 1)
    q_r = cr[None] * qa_r[:, :, None, :] - ci[None] * qa_i[:, :, None, :]
    q_i = cr[None] * qa_i[:, :, None, :] + ci[None] * qa_r[:, :, None, :]

    def state_out(w):
        w = w.reshape(LAG, n_pairs, gpp, n_ch, n_state)
        return jnp.einsum('dkgop,gf->kgpdfo', w, eye).reshape(n_pairs, gpp * n_state, LAG * PAIR_CH)

    wko = jnp.concatenate([wk, state_out(q_r), state_out(-q_i)], axis=1)
    return ws.astype(jnp.bfloat16), wko.astype(jnp.bfloat16)


def _chunk_prefix_matrix():
    r = jnp.arange(CHUNK_BLOCK + HALO_ROWS)[:, None]
    c = jnp.arange(CHUNK_BLOCK)[None, :]
    return ((c < r) & (r <= CHUNK_BLOCK)).astype(jnp.bfloat16)


def kernel(x, norm_mix, norm_ffn, norm_final, s5_a_re, s5_a_im, s5_log_dt, s5_b_re, s5_b_im,
           s5_c_re, s5_c_im, s5_d, s5_w_glu, sc_w_in, sc_conv_w, sc_w_out,
           ffn_w_up, ffn_conv_w, ffn_conv_b, ffn_w_down):
    bf16 = jnp.bfloat16
    row = lambda v: v.reshape(1, -1)

    rate_r, rate_i, bb_r, bb_i = _s5_discretise(s5_a_re[0], s5_a_im[0], s5_log_dt[0],
                                                 s5_b_re[0], s5_b_im[0])
    tabs = _s5_tables(rate_r, rate_i)
    ws, wko = _s5_weights(rate_r, rate_i, bb_r, bb_i, s5_c_re[0], s5_c_im[0])
    h = _s5_layer(x, row(norm_mix[0]), tabs, _chunk_prefix_matrix(), ws, wko, row(s5_d[0]),
                  s5_w_glu[0].astype(bf16), tile_rows=512)
    h = _ffn_layer(h, row(norm_ffn[0]), ffn_w_up[0].astype(bf16), ffn_conv_w[0],
                   row(ffn_conv_b[0]), ffn_w_down[0].astype(bf16), row(norm_final),
                   tile_rows=1024, final_norm=False)
    h = _shortconv_layer(h, row(norm_mix[1]), sc_w_in[0].astype(bf16), sc_conv_w[0],
                         sc_w_out[0].astype(bf16), tile_rows=1024)
    h = _ffn_layer(h, row(norm_ffn[1]), ffn_w_up[1].astype(bf16), ffn_conv_w[1],
                   row(ffn_conv_b[1]), ffn_w_down[1].astype(bf16), row(norm_final),
                   tile_rows=1024, final_norm=True)
    return h
```

```python
import functools

import jax
import jax.numpy as jnp
from jax import lax
from jax.experimental import pallas as pl
from jax.experimental.pallas import tpu as pltpu

RMS_EPS = 1e-6
S5_GROUP = 16
S5_STATE = 64
CONV_WIDTH = 3
HALO_ROWS = 8

LANES = 128
LAG = 8
PAIR_CH = 32
CHUNK_BLOCK = 16
COL_BLOCK = 256
ROW_BLOCK = 512
VMEM_LIMIT_BYTES = 56 * 1024 * 1024


def _rmsnorm(x, gain):
    ms = jnp.mean(x * x, axis=-1, keepdims=True)
    return x * lax.rsqrt(ms + RMS_EPS) * gain


def _pack_rows(v):
    return pltpu.bitcast(v.astype(jnp.bfloat16), jnp.uint32)


def _const_spec(shape):
    nd = len(shape)
    return pl.BlockSpec(shape, lambda b, i: (0,) * nd, pipeline_mode=pl.Buffered(1))


def _tile_spec(tile_rows, d):
    return pl.BlockSpec((None, tile_rows, d), lambda b, i: (b, i, 0))


def _compiler_params():
    return pltpu.CompilerParams(
        dimension_semantics=("arbitrary", "arbitrary"),
        vmem_limit_bytes=VMEM_LIMIT_BYTES,
    )


def _s5_kernel(h_ref, gain_ref, er_ref, ei_ref, fr_ref, fi_ref, pr_ref, pi_ref,
               tri_ref, ws_ref, wko_ref, dskip_ref, wglu_ref, o_ref,
               carry_ref, us_ref, ys_ref, hp_ref):
    @pl.when(pl.program_id(1) == 0)
    def _():
        carry_ref[...] = jnp.zeros_like(carry_ref)

    bf16 = jnp.bfloat16
    f32 = jnp.float32
    tile_rows, d = h_ref.shape
    n_lane_tiles = d // LANES
    pairs_per_lane_tile = LANES // PAIR_CH
    n_pairs = d // PAIR_CH
    pair_modes = (PAIR_CH // S5_GROUP) * S5_STATE
    block_rows = min(ROW_BLOCK, tile_rows)
    n_chunks = block_rows // LAG
    tri = tri_ref[...]
    er, ei, fr, fi = er_ref[...], ei_ref[...], fr_ref[...], fi_ref[...]
    pr, pi = pr_ref[...], pi_ref[...]

    for rb in range(tile_rows // block_rows):
        r0 = rb * block_rows
        c0 = rb * n_chunks
        x = h_ref[r0:r0 + block_rows, :]
        u = _rmsnorm(x, gain_ref[...])
        for j in range(n_lane_tiles):
            us_ref[j, r0:r0 + block_rows, :] = u[:, j * LANES:(j + 1) * LANES]
        lag_rows = [[_pack_rows(us_ref[j, pl.ds(r0 + dd, n_chunks, stride=LAG), :])
                     for j in range(n_lane_tiles)] for dd in range(LAG)]

        u2, s_re, s_im = [], [], []
        for k in range(n_pairs):
            j, q = divmod(k, pairs_per_lane_tile)
            lanes = slice(q * PAIR_CH, (q + 1) * PAIR_CH)
            u2k = pltpu.bitcast(
                jnp.concatenate([lag_rows[dd][j][:, lanes] for dd in range(LAG)], axis=1), bf16)
            u2.append(u2k)
            sk = jnp.dot(u2k, ws_ref[k], preferred_element_type=f32)
            s_re.append(sk[:, :pair_modes])
            s_im.append(sk[:, pair_modes:])
        s_re = jnp.concatenate(s_re, axis=1)
        s_im = jnp.concatenate(s_im, axis=1)

        for blk in range(n_chunks // CHUNK_BLOCK):
            rows = slice(blk * CHUNK_BLOCK, (blk + 1) * CHUNK_BLOCK)
            hrows = slice(c0 + blk * CHUNK_BLOCK, c0 + (blk + 1) * CHUNK_BLOCK)
            sr = s_re[rows, :].astype(bf16)
            si = s_im[rows, :].astype(bf16)
            xr = jnp.dot(tri, er * sr - ei * si, preferred_element_type=f32)
            xi = jnp.dot(tri, er * si + ei * sr, preferred_element_type=f32)
            c_r, c_i = carry_ref[0:1, :], carry_ref[1:2, :]
            gr = (xr[:CHUNK_BLOCK, :] + c_r).astype(bf16)
            gi = (xi[:CHUNK_BLOCK, :] + c_i).astype(bf16)
            hpr = fr * gr - fi * gi
            hpi = fr * gi + fi * gr
            for k in range(n_pairs):
                md = slice(k * pair_modes, (k + 1) * pair_modes)
                hp_ref[hrows, 2 * k * pair_modes:(2 * k + 1) * pair_modes] = hpr[:, md]
                hp_ref[hrows, (2 * k + 1) * pair_modes:(2 * k + 2) * pair_modes] = hpi[:, md]
            t_r = xr[CHUNK_BLOCK:CHUNK_BLOCK + 1, :] + c_r
            t_i = xi[CHUNK_BLOCK:CHUNK_BLOCK + 1, :] + c_i
            carry_ref[0:1, :] = pr * t_r - pi * t_i
            carry_ref[1:2, :] = pr * t_i + pi * t_r

        y2 = []
        for k in range(n_pairs):
            hp = hp_ref[c0:c0 + n_chunks, 2 * k * pair_modes:(2 * k + 2) * pair_modes]
            y2.append(_pack_rows(jnp.dot(jnp.concatenate([u2[k], hp], axis=1), wko_ref[k],
                                         preferred_element_type=f32)))
        for dd in range(LAG):
            lanes = slice(dd * PAIR_CH, (dd + 1) * PAIR_CH)
            for j in range(n_lane_tiles):
                piece = jnp.concatenate(
                    [y2[j * pairs_per_lane_tile + q][:, lanes] for q in range(pairs_per_lane_tile)],
                    axis=1)
                ys_ref[j, pl.ds(r0 + dd, n_chunks, stride=LAG), :] = (
                    pltpu.bitcast(piece, bf16).astype(f32))
        y = jnp.concatenate([ys_ref[j, r0:r0 + block_rows, :] for j in range(n_lane_tiles)], axis=1)

        y = y + dskip_ref[...] * u
        z = jax.nn.gelu(y).astype(bf16)
        for c in range(d // COL_BLOCK):
            cols = slice(c * COL_BLOCK, (c + 1) * COL_BLOCK)
            gcols = slice(d + c * COL_BLOCK, d + (c + 1) * COL_BLOCK)
            za = jnp.dot(z, wglu_ref[:, cols], preferred_element_type=f32)
            zg = jnp.dot(z, wglu_ref[:, gcols], preferred_element_type=f32)
            o_ref[r0:r0 + block_rows, cols] = x[:, cols] + za * jax.nn.sigmoid(zg)


def _s5_layer(h, gain, tabs, tri, ws, wko, dskip, wglu, *, tile_rows):
    bsz, seq, d = h.shape
    er, ei, fr, fi, pr, pi = tabs
    n_modes = er.shape[1]
    return pl.pallas_call(
        _s5_kernel,
        grid=(bsz, seq // tile_rows),
        in_specs=[
            _tile_spec(tile_rows, d),
            _const_spec(gain.shape),
            _const_spec(er.shape), _const_spec(ei.shape),
            _const_spec(fr.shape), _const_spec(fi.shape),
            _const_spec(pr.shape), _const_spec(pi.shape),
            _const_spec(tri.shape),
            _const_spec(ws.shape), _const_spec(wko.shape),
            _const_spec(dskip.shape), _const_spec(wglu.shape),
        ],
        out_specs=_tile_spec(tile_rows, d),
        out_shape=jax.ShapeDtypeStruct(h.shape, h.dtype),
        scratch_shapes=[
            pltpu.VMEM((HALO_ROWS, n_modes), jnp.float32),
            pltpu.VMEM((d // LANES, tile_rows, LANES), jnp.float32),
            pltpu.VMEM((d // LANES, tile_rows, LANES), jnp.float32),
            pltpu.VMEM((tile_rows // LAG, 2 * n_modes), jnp.bfloat16),
        ],
        compiler_params=_compiler_params(),
        name="s5_mixer",
    )(h, gain, er, ei, fr, fi, pr, pi, tri, ws, wko, dskip, wglu)


def _causal_conv_rows(val, halo_ref, w_ref):
    tile_rows = val.shape[0]
    ext = jnp.concatenate([halo_ref[...], val], axis=0)
    out = val * w_ref[CONV_WIDTH - 1:CONV_WIDTH, :]
    for k in range(CONV_WIDTH - 1):
        shift = CONV_WIDTH - 1 - k
        out = out + ext[HALO_ROWS - shift:HALO_ROWS - shift + tile_rows, :] * w_ref[k:k + 1, :]
    halo_ref[...] = val[tile_rows - HALO_ROWS:, :]
    return out


def _zero_halo_at_sequence_start(halo_ref):
    @pl.when(pl.program_id(1) == 0)
    def _():
        halo_ref[...] = jnp.zeros_like(halo_ref)


def _ffn_kernel(h_ref, gain_ref, wup_ref, cw_ref, cb_ref, wdown_ref, fgain_ref, o_ref,
                halo_ref, *, d_ff, final_norm):
    _zero_halo_at_sequence_start(halo_ref)
    for r in range(h_ref.shape[0] // ROW_BLOCK):
        rows = slice(r * ROW_BLOCK, (r + 1) * ROW_BLOCK)
        x = h_ref[rows, :]
        u = _rmsnorm(x, gain_ref[...]).astype(jnp.bfloat16)
        g = jnp.dot(u, wup_ref[:, :d_ff], preferred_element_type=jnp.float32)
        v = jnp.dot(u, wup_ref[:, d_ff:], preferred_element_type=jnp.float32)
        g = _causal_conv_rows(g, halo_ref, cw_ref) + cb_ref[...]
        act = (jax.nn.silu(g) * v).astype(jnp.bfloat16)
        out = x + jnp.dot(act, wdown_ref[...], preferred_element_type=jnp.float32)
        if final_norm:
            out = _rmsnorm(out, fgain_ref[...])
        o_ref[rows, :] = out


def _ffn_layer(h, gain, wup, cw, cb, wdown, fgain, *, tile_rows, final_norm):
    bsz, seq, d = h.shape
    d_ff = wdown.shape[0]
    kernel = functools.partial(_ffn_kernel, d_ff=d_ff, final_norm=final_norm)
    return pl.pallas_call(
        kernel,
        grid=(bsz, seq // tile_rows),
        in_specs=[
            _tile_spec(tile_rows, d),
            _const_spec(gain.shape),
            _const_spec(wup.shape), _const_spec(cw.shape), _const_spec(cb.shape),
            _const_spec(wdown.shape), _const_spec(fgain.shape),
        ],
        out_specs=_tile_spec(tile_rows, d),
        out_shape=jax.ShapeDtypeStruct(h.shape, h.dtype),
        scratch_shapes=[pltpu.VMEM((HALO_ROWS, d_ff), jnp.float32)],
        compiler_params=_compiler_params(),
        name="conv_ffn_final" if final_norm else "conv_ffn",
    )(h, gain, wup, cw, cb, wdown, fgain)


def _shortconv_kernel(h_ref, gain_ref, win_ref, cw_ref, wout_ref, o_ref, halo_ref):
    _zero_halo_at_sequence_start(halo_ref)
    d = h_ref.shape[-1]
    for r in range(h_ref.shape[0] // ROW_BLOCK):
        rows = slice(r * ROW_BLOCK, (r + 1) * ROW_BLOCK)
        x = h_ref[rows, :]
        u = _rmsnorm(x, gain_ref[...]).astype(jnp.bfloat16)
        b_gate, c_gate, hh = [
            jnp.dot(u, win_ref[:, k * d:(k + 1) * d], preferred_element_type=jnp.float32)
            for k in range(3)]
        v = _causal_conv_rows(c_gate * hh, halo_ref, cw_ref)
        o_ref[rows, :] = x + jnp.dot((b_gate * v).astype(jnp.bfloat16), wout_ref[...],
                                     preferred_element_type=jnp.float32)


def _shortconv_layer(h, gain, win, cw, wout, *, tile_rows):
    bsz, seq, d = h.shape
    return pl.pallas_call(
        _shortconv_kernel,
        grid=(bsz, seq // tile_rows),
        in_specs=[
            _tile_spec(tile_rows, d),
            _const_spec(gain.shape),
            _const_spec(win.shape), _const_spec(cw.shape), _const_spec(wout.shape),
        ],
        out_specs=_tile_spec(tile_rows, d),
        out_shape=jax.ShapeDtypeStruct(h.shape, h.dtype),
        scratch_shapes=[pltpu.VMEM((HALO_ROWS, d), jnp.float32)],
        compiler_params=_compiler_params(),
        name="shortconv_mixer",
    )(h, gain, win, cw, wout)


def _s5_discretise(a_re, a_im, log_dt, b_re, b_im):
    f32 = jnp.float32
    lam_r = a_re.astype(f32)
    lam_i = a_im.astype(f32)
    dt = jnp.exp(log_dt.astype(f32))[:, None]
    mag = jnp.exp(lam_r * dt)
    ab_r = mag * jnp.cos(lam_i * dt)
    ab_i = mag * jnp.sin(lam_i * dt)
    den = lam_r * lam_r + lam_i * lam_i
    nr = ab_r - 1.0
    g_r = ((nr * lam_r + ab_i * lam_i) / den)[..., None]
    g_i = ((ab_i * lam_r - nr * lam_i) / den)[..., None]
    br = b_re.astype(f32)
    bi = b_im.astype(f32)
    return lam_r * dt, lam_i * dt, g_r * br - g_i * bi, g_r * bi + g_i * br


def _a_power(rate_r, rate_i, k):
    k = jnp.asarray(k, jnp.float32)
    mag = jnp.exp(rate_r * k)
    return mag * jnp.cos(rate_i * k), mag * jnp.sin(rate_i * k)


def _cmul(ar, ai, br, bi):
    return ar * br - ai * bi, ar * bi + ai * br


def _s5_tables(rate_r, rate_i):
    rr = rate_r.reshape(1, -1)
    ri = rate_i.reshape(1, -1)
    mid = CHUNK_BLOCK // 2
    c = jnp.arange(CHUNK_BLOCK)[:, None]
    er, ei = _a_power(rr, ri, LAG * (mid - c))
    fr, fi = _a_power(rr, ri, LAG * (c - 1 - mid))
    pr, pi = _a_power(rr, ri, LAG * CHUNK_BLOCK)
    bf16 = jnp.bfloat16
    return er.astype(bf16), ei.astype(bf16), fr.astype(bf16), fi.astype(bf16), pr, pi


def _s5_weights(rate_r, rate_i, bb_r, bb_i, c_re, c_im):
    f32 = jnp.float32
    hp = lax.Precision.HIGHEST
    n_groups, n_state, n_ch = bb_r.shape
    gpp = PAIR_CH // n_ch
    n_pairs = n_groups // gpp
    wide = LAG * PAIR_CH
    lane = jnp.arange(wide)
    lane_lag, lane_g2, lane_ch = lane // PAIR_CH, (lane // n_ch) % gpp, lane % n_ch
    row = jnp.arange(LAG * n_ch)
    row_lag, row_ch = row // n_ch, row % n_ch
    ch = jnp.arange(n_ch)
    spread_cols = (ch[:, None] == lane_ch[None, :]).astype(f32)
    spread_rows = (row_ch[:, None] == ch[None, :]).astype(f32)
    group_g2 = jnp.arange(n_groups) % gpp
    own_cols = (group_g2[:, None] == lane_g2[None, :]).astype(f32)[:, None, :]
    causal = (lane_lag[None, :] >= row_lag[:, None]).astype(f32)[None]

    cw_r = jnp.einsum('gop,oc->gpc', c_re.astype(f32), spread_cols, precision=hp)
    cw_i = jnp.einsum('gop,oc->gpc', c_im.astype(f32), spread_cols, precision=hp)
    rout_r, rout_i = _cmul(*_a_power(rate_r[..., None], rate_i[..., None], lane_lag), cw_r, cw_i)
    bt_r = jnp.einsum('rh,gph->grp', spread_rows, bb_r, precision=hp)
    bt_i = jnp.einsum('rh,gph->grp', spread_rows, bb_i, precision=hp)
    lin_r, lin_i = _cmul(*_a_power(rate_r[:, None, :], rate_i[:, None, :], -row_lag[None, :, None]),
                         bt_r, bt_i)

    wk_g = (jnp.einsum('grp,gpc->grc', lin_r, rout_r, precision=hp)
            - jnp.einsum('grp,gpc->grc', lin_i, rout_i, precision=hp)) * causal * own_cols
    ws_r, ws_i = _cmul(*_a_power(rate_r[:, None, :], rate_i[:, None, :], LAG - 1), lin_r, lin_i)
    wo_r, wo_i = _cmul(*_a_power(rate_r[..., None], rate_i[..., None], 1), rout_r, rout_i)

    own_group = [(group_g2 == g2).astype(f32)[:, None, None] for g2 in range(gpp)]
    ws_g = jnp.concatenate([w * m for w in (ws_r, ws_i) for m in own_group], axis=-1)

    def rows_by_lag(w):
        w = w.reshape(n_pairs, gpp, LAG, n_ch, w.shape[-1])
        return jnp.transpose(w, (0, 2, 1, 3, 4)).reshape(n_pairs, wide, w.shape[-1])

    def rows_by_mode(w):
        return (w * own_cols).reshape(n_pairs, gpp * n_state, w.shape[-1])

    ws = rows_by_lag(ws_g)
    wko = jnp.concatenate([rows_by_lag(wk_g), rows_by_mode(wo_r), rows_by_mode(-wo_i)], axis=1)
    return ws.astype(jnp.bfloat16), wko.astype(jnp.bfloat16)


def _chunk_prefix_matrix():
    r = jnp.arange(CHUNK_BLOCK + HALO_ROWS)[:, None]
    c = jnp.arange(CHUNK_BLOCK)[None, :]
    return ((c < r) & (r <= CHUNK_BLOCK)).astype(jnp.bfloat16)


def kernel(x, norm_mix, norm_ffn, norm_final, s5_a_re, s5_a_im, s5_log_dt, s5_b_re, s5_b_im,
           s5_c_re, s5_c_im, s5_d, s5_w_glu, sc_w_in, sc_conv_w, sc_w_out,
           ffn_w_up, ffn_conv_w, ffn_conv_b, ffn_w_down):
    bf16 = jnp.bfloat16
    row = lambda v: v.reshape(1, -1)

    rate_r, rate_i, bb_r, bb_i = _s5_discretise(s5_a_re[0], s5_a_im[0], s5_log_dt[0],
                                                 s5_b_re[0], s5_b_im[0])
    tabs = _s5_tables(rate_r, rate_i)
    ws, wko = _s5_weights(rate_r, rate_i, bb_r, bb_i, s5_c_re[0], s5_c_im[0])
    h = _s5_layer(x, row(norm_mix[0]), tabs, _chunk_prefix_matrix(), ws, wko, row(s5_d[0]),
                  s5_w_glu[0].astype(bf16), tile_rows=1024)
    h = _ffn_layer(h, row(norm_ffn[0]), ffn_w_up[0].astype(bf16), ffn_conv_w[0],
                   row(ffn_conv_b[0]), ffn_w_down[0].astype(bf16), row(norm_final),
                   tile_rows=1024, final_norm=False)
    h = _shortconv_layer(h, row(norm_mix[1]), sc_w_in[0].astype(bf16), sc_conv_w[0],
                         sc_w_out[0].astype(bf16), tile_rows=1024)
    h = _ffn_layer(h, row(norm_ffn[1]), ffn_w_up[1].astype(bf16), ffn_conv_w[1],
                   row(ffn_conv_b[1]), ffn_w_down[1].astype(bf16), row(norm_final),
                   tile_rows=1024, final_norm=True)
    return h
```

```python
import functools

import jax
import jax.numpy as jnp
from jax import lax
from jax.experimental import pallas as pl
from jax.experimental.pallas import tpu as pltpu

RMS_EPS = 1e-6
S5_GROUP = 16
S5_STATE = 64
CONV_WIDTH = 3
HALO_ROWS = 8

LANES = 128
BF16_SUBLANES = 16
LAG = 8
PAIR_CH = 32
CHUNK_BLOCK = 16
COL_BLOCK = 256
ROW_BLOCK = 512
VMEM_LIMIT_BYTES = 56 * 1024 * 1024


def _rmsnorm(x, gain):
    ms = jnp.mean(x * x, axis=-1, keepdims=True)
    return x * lax.rsqrt(ms + RMS_EPS) * gain


def _pack_rows(v):
    return pltpu.bitcast(v.astype(jnp.bfloat16), jnp.uint32)


def _const_spec(shape):
    nd = len(shape)
    return pl.BlockSpec(shape, lambda b, i: (0,) * nd, pipeline_mode=pl.Buffered(1))


def _tile_spec(tile_rows, d):
    return pl.BlockSpec((None, tile_rows, d), lambda b, i: (b, i, 0))


def _compiler_params():
    return pltpu.CompilerParams(
        dimension_semantics=("arbitrary", "arbitrary"),
        vmem_limit_bytes=VMEM_LIMIT_BYTES,
    )


def _cast_blocks(n_rows, n_steps):
    for n_blocks in range(n_steps, 0, -1):
        if (n_steps % n_blocks == 0 and n_rows % n_blocks == 0
                and (n_rows // n_blocks) % BF16_SUBLANES == 0):
            return n_blocks
    raise ValueError(f"no row blocking of {n_rows} rows over {n_steps} steps")


def _sublayer_call(body, name, h, consts, scratch_shapes, *, tile_rows, cast_next=()):
    bsz, seq, d = h.shape
    n_tiles = seq // tile_rows
    n_steps = bsz * n_tiles
    n_in = 1 + len(consts)
    n_cast = len(cast_next)

    def cast_specs(w, layer):
        _, n_rows, n_cols = w.shape
        n_blocks = _cast_blocks(n_rows, n_steps)
        steps_per_block = n_steps // n_blocks
        block = lambda b, i: (b * n_tiles + i) // steps_per_block
        return (pl.BlockSpec((None, n_rows // n_blocks, n_cols), lambda b, i: (layer, block(b, i), 0)),
                pl.BlockSpec((n_rows // n_blocks, n_cols), lambda b, i: (block(b, i), 0)))

    def kernel(*refs):
        ins, cast_ins = refs[:n_in], refs[n_in:n_in + n_cast]
        out, cast_outs = refs[n_in + n_cast], refs[n_in + n_cast + 1:n_in + 2 * n_cast + 1]
        scratch = refs[n_in + 2 * n_cast + 1:]
        for src, dst in zip(cast_ins, cast_outs):
            dst[...] = src[...].astype(dst.dtype)
        body(*ins, out, *scratch)

    cast_in, cast_out = zip(*[cast_specs(w, layer) for w, layer in cast_next]) if cast_next else ((), ())
    outs = pl.pallas_call(
        kernel,
        grid=(bsz, n_tiles),
        in_specs=[_tile_spec(tile_rows, d)] + [_const_spec(c.shape) for c in consts] + list(cast_in),
        out_specs=[_tile_spec(tile_rows, d)] + list(cast_out),
        out_shape=[jax.ShapeDtypeStruct(h.shape, h.dtype)]
        + [jax.ShapeDtypeStruct(w.shape[1:], jnp.bfloat16) for w, _ in cast_next],
        scratch_shapes=scratch_shapes,
        compiler_params=_compiler_params(),
        name=name,
    )(h, *consts, *[w for w, _ in cast_next])
    return outs[0], tuple(outs[1:])


def _s5_kernel(h_ref, gain_ref, er_ref, ei_ref, fr_ref, fi_ref, pr_ref, pi_ref,
               tri_ref, ws_ref, wko_ref, dskip_ref, wglu_ref, o_ref,
               carry_ref, us_ref, ys_ref, hp_ref):
    @pl.when(pl.program_id(1) == 0)
    def _():
        carry_ref[...] = jnp.zeros_like(carry_ref)

    bf16 = jnp.bfloat16
    f32 = jnp.float32
    tile_rows, d = h_ref.shape
    n_lane_tiles = d // LANES
    pairs_per_lane_tile = LANES // PAIR_CH
    n_pairs = d // PAIR_CH
    pair_modes = (PAIR_CH // S5_GROUP) * S5_STATE
    block_rows = min(ROW_BLOCK, tile_rows)
    n_chunks = block_rows // LAG
    tri = tri_ref[...]
    er, ei, fr, fi = er_ref[...], ei_ref[...], fr_ref[...], fi_ref[...]
    pr, pi = pr_ref[...], pi_ref[...]

    for rb in range(tile_rows // block_rows):
        r0 = rb * block_rows
        c0 = rb * n_chunks
        x = h_ref[r0:r0 + block_rows, :]
        u = _rmsnorm(x, gain_ref[...])
        for j in range(n_lane_tiles):
            us_ref[j, r0:r0 + block_rows, :] = u[:, j * LANES:(j + 1) * LANES]
        lag_rows = [[_pack_rows(us_ref[j, pl.ds(r0 + dd, n_chunks, stride=LAG), :])
                     for j in range(n_lane_tiles)] for dd in range(LAG)]

        u2, s_re, s_im = [], [], []
        for k in range(n_pairs):
            j, q = divmod(k, pairs_per_lane_tile)
            lanes = slice(q * PAIR_CH, (q + 1) * PAIR_CH)
            u2k = pltpu.bitcast(
                jnp.concatenate([lag_rows[dd][j][:, lanes] for dd in range(LAG)], axis=1), bf16)
            u2.append(u2k)
            sk = jnp.dot(u2k, ws_ref[k], preferred_element_type=f32)
            s_re.append(sk[:, :pair_modes])
            s_im.append(sk[:, pair_modes:])
        s_re = jnp.concatenate(s_re, axis=1)
        s_im = jnp.concatenate(s_im, axis=1)

        for blk in range(n_chunks // CHUNK_BLOCK):
            rows = slice(blk * CHUNK_BLOCK, (blk + 1) * CHUNK_BLOCK)
            hrows = slice(c0 + blk * CHUNK_BLOCK, c0 + (blk + 1) * CHUNK_BLOCK)
            sr = s_re[rows, :].astype(bf16)
            si = s_im[rows, :].astype(bf16)
            xr = jnp.dot(tri, er * sr - ei * si, preferred_element_type=f32)
            xi = jnp.dot(tri, er * si + ei * sr, preferred_element_type=f32)
            c_r, c_i = carry_ref[0:1, :], carry_ref[1:2, :]
            gr = (xr[:CHUNK_BLOCK, :] + c_r).astype(bf16)
            gi = (xi[:CHUNK_BLOCK, :] + c_i).astype(bf16)
            hpr = fr * gr - fi * gi
            hpi = fr * gi + fi * gr
            for k in range(n_pairs):
                md = slice(k * pair_modes, (k + 1) * pair_modes)
                hp_ref[hrows, 2 * k * pair_modes:(2 * k + 1) * pair_modes] = hpr[:, md]
                hp_ref[hrows, (2 * k + 1) * pair_modes:(2 * k + 2) * pair_modes] = hpi[:, md]
            t_r = xr[CHUNK_BLOCK:CHUNK_BLOCK + 1, :] + c_r
            t_i = xi[CHUNK_BLOCK:CHUNK_BLOCK + 1, :] + c_i
            carry_ref[0:1, :] = pr * t_r - pi * t_i
            carry_ref[1:2, :] = pr * t_i + pi * t_r

        y2 = []
        for k in range(n_pairs):
            hp = hp_ref[c0:c0 + n_chunks, 2 * k * pair_modes:(2 * k + 2) * pair_modes]
            y2.append(_pack_rows(jnp.dot(jnp.concatenate([u2[k], hp], axis=1), wko_ref[k],
                                         preferred_element_type=f32)))
        for dd in range(LAG):
            lanes = slice(dd * PAIR_CH, (dd + 1) * PAIR_CH)
            for j in range(n_lane_tiles):
                piece = jnp.concatenate(
                    [y2[j * pairs_per_lane_tile + q][:, lanes] for q in range(pairs_per_lane_tile)],
                    axis=1)
                ys_ref[j, pl.ds(r0 + dd, n_chunks, stride=LAG), :] = (
                    pltpu.bitcast(piece, bf16).astype(f32))
        y = jnp.concatenate([ys_ref[j, r0:r0 + block_rows, :] for j in range(n_lane_tiles)], axis=1)

        y = y + dskip_ref[...] * u
        z = jax.nn.gelu(y).astype(bf16)
        for c in range(d // COL_BLOCK):
            cols = slice(c * COL_BLOCK, (c + 1) * COL_BLOCK)
            gcols = slice(d + c * COL_BLOCK, d + (c + 1) * COL_BLOCK)
            za = jnp.dot(z, wglu_ref[:, cols], preferred_element_type=f32)
            zg = jnp.dot(z, wglu_ref[:, gcols], preferred_element_type=f32)
            o_ref[r0:r0 + block_rows, cols] = x[:, cols] + za * jax.nn.sigmoid(zg)


def _s5_layer(h, gain, tabs, tri, ws, wko, dskip, wglu, *, tile_rows, cast_next):
    d = h.shape[-1]
    n_modes = tabs[0].shape[1]
    scratch = [
        pltpu.VMEM((HALO_ROWS, n_modes), jnp.float32),
        pltpu.VMEM((d // LANES, tile_rows, LANES), jnp.float32),
        pltpu.VMEM((d // LANES, tile_rows, LANES), jnp.float32),
        pltpu.VMEM((tile_rows // LAG, 2 * n_modes), jnp.bfloat16),
    ]
    return _sublayer_call(_s5_kernel, "s5_mixer", h, (gain, *tabs, tri, ws, wko, dskip, wglu),
                          scratch, tile_rows=tile_rows, cast_next=cast_next)


def _causal_conv_rows(val, halo_ref, w_ref):
    tile_rows = val.shape[0]
    ext = jnp.concatenate([halo_ref[...], val], axis=0)
    out = val * w_ref[CONV_WIDTH - 1:CONV_WIDTH, :]
    for k in range(CONV_WIDTH - 1):
        shift = CONV_WIDTH - 1 - k
        out = out + ext[HALO_ROWS - shift:HALO_ROWS - shift + tile_rows, :] * w_ref[k:k + 1, :]
    halo_ref[...] = val[tile_rows - HALO_ROWS:, :]
    return out


def _zero_halo_at_sequence_start(halo_ref):
    @pl.when(pl.program_id(1) == 0)
    def _():
        halo_ref[...] = jnp.zeros_like(halo_ref)


def _ffn_kernel(h_ref, gain_ref, wup_ref, cw_ref, cb_ref, wdown_ref, fgain_ref, o_ref,
                halo_ref, *, d_ff, final_norm):
    _zero_halo_at_sequence_start(halo_ref)
    for r in range(h_ref.shape[0] // ROW_BLOCK):
        rows = slice(r * ROW_BLOCK, (r + 1) * ROW_BLOCK)
        x = h_ref[rows, :]
        u = _rmsnorm(x, gain_ref[...]).astype(jnp.bfloat16)
        g = jnp.dot(u, wup_ref[:, :d_ff], preferred_element_type=jnp.float32)
        v = jnp.dot(u, wup_ref[:, d_ff:], preferred_element_type=jnp.float32)
        g = _causal_conv_rows(g, halo_ref, cw_ref) + cb_ref[...]
        act = (jax.nn.silu(g) * v).astype(jnp.bfloat16)
        out = x + jnp.dot(act, wdown_ref[...], preferred_element_type=jnp.float32)
        if final_norm:
            out = _rmsnorm(out, fgain_ref[...])
        o_ref[rows, :] = out


def _ffn_layer(h, gain, wup, cw, cb, wdown, fgain, *, tile_rows, final_norm, cast_next=()):
    d_ff = wdown.shape[0]
    body = functools.partial(_ffn_kernel, d_ff=d_ff, final_norm=final_norm)
    scratch = [pltpu.VMEM((HALO_ROWS, d_ff), jnp.float32)]
    return _sublayer_call(body, "conv_ffn_final" if final_norm else "conv_ffn", h,
                          (gain, wup, cw, cb, wdown, fgain), scratch,
                          tile_rows=tile_rows, cast_next=cast_next)


def _shortconv_kernel(h_ref, gain_ref, win_ref, cw_ref, wout_ref, o_ref, halo_ref):
    _zero_halo_at_sequence_start(halo_ref)
    d = h_ref.shape[-1]
    for r in range(h_ref.shape[0] // ROW_BLOCK):
        rows = slice(r * ROW_BLOCK, (r + 1) * ROW_BLOCK)
        x = h_ref[rows, :]
        u = _rmsnorm(x, gain_ref[...]).astype(jnp.bfloat16)
        b_gate, c_gate, hh = [
            jnp.dot(u, win_ref[:, k * d:(k + 1) * d], preferred_element_type=jnp.float32)
            for k in range(3)]
        v = _causal_conv_rows(c_gate * hh, halo_ref, cw_ref)
        o_ref[rows, :] = x + jnp.dot((b_gate * v).astype(jnp.bfloat16), wout_ref[...],
                                     preferred_element_type=jnp.float32)


def _shortconv_layer(h, gain, win, cw, wout, *, tile_rows, cast_next):
    scratch = [pltpu.VMEM((HALO_ROWS, h.shape[-1]), jnp.float32)]
    return _sublayer_call(_shortconv_kernel, "shortconv_mixer", h, (gain, win, cw, wout), scratch,
                          tile_rows=tile_rows, cast_next=cast_next)


def _s5_discretise(a_re, a_im, log_dt, b_re, b_im):
    f32 = jnp.float32
    lam_r = a_re.astype(f32)
    lam_i = a_im.astype(f32)
    dt = jnp.exp(log_dt.astype(f32))[:, None]
    mag = jnp.exp(lam_r * dt)
    ab_r = mag * jnp.cos(lam_i * dt)
    ab_i = mag * jnp.sin(lam_i * dt)
    den = lam_r * lam_r + lam_i * lam_i
    nr = ab_r - 1.0
    g_r = ((nr * lam_r + ab_i * lam_i) / den)[..., None]
    g_i = ((ab_i * lam_r - nr * lam_i) / den)[..., None]
    br = b_re.astype(f32)
    bi = b_im.astype(f32)
    return lam_r * dt, lam_i * dt, g_r * br - g_i * bi, g_r * bi + g_i * br


def _a_power(rate_r, rate_i, k):
    k = jnp.asarray(k, jnp.float32)
    mag = jnp.exp(rate_r * k)
    return mag * jnp.cos(rate_i * k), mag * jnp.sin(rate_i * k)


def _cmul(ar, ai, br, bi):
    return ar * br - ai * bi, ar * bi + ai * br


def _s5_tables(rate_r, rate_i):
    rr = rate_r.reshape(1, -1)
    ri = rate_i.reshape(1, -1)
    mid = CHUNK_BLOCK // 2
    c = jnp.arange(CHUNK_BLOCK)[:, None]
    er, ei = _a_power(rr, ri, LAG * (mid - c))
    fr, fi = _a_power(rr, ri, LAG * (c - 1 - mid))
    pr, pi = _a_power(rr, ri, LAG * CHUNK_BLOCK)
    bf16 = jnp.bfloat16
    return er.astype(bf16), ei.astype(bf16), fr.astype(bf16), fi.astype(bf16), pr, pi


def _s5_weights(rate_r, rate_i, bb_r, bb_i, c_re, c_im):
    f32 = jnp.float32
    hp = lax.Precision.HIGHEST
    n_groups, n_state, n_ch = bb_r.shape
    gpp = PAIR_CH // n_ch
    n_pairs = n_groups // gpp
    wide = LAG * PAIR_CH
    pair_modes = gpp * n_state
    lane = jnp.arange(wide)
    lane_lag, lane_g2, lane_ch = lane // PAIR_CH, (lane // n_ch) % gpp, lane % n_ch
    mode_g2 = jnp.arange(pair_modes) // n_state
    ch = jnp.arange(n_ch)
    spread = (ch[:, None] == lane_ch[None, :]).astype(f32)
    own = (mode_g2[:, None] == lane_g2[None, :]).astype(f32)
    causal = (lane_lag[None, :] >= lane_lag[:, None]).astype(f32)
    pr = rate_r.reshape(n_pairs, pair_modes)
    pi = rate_i.reshape(n_pairs, pair_modes)

    def pair_rows(w):
        return w.reshape((n_pairs, pair_modes) + w.shape[2:])

    cw_r = jnp.einsum('gop,oc->gpc', c_re.astype(f32), spread, precision=hp)
    cw_i = jnp.einsum('gop,oc->gpc', c_im.astype(f32), spread, precision=hp)
    rout_r, rout_i = _cmul(*_a_power(pr[..., None], pi[..., None], lane_lag),
                           pair_rows(cw_r) * own, pair_rows(cw_i) * own)
    bt_r = jnp.einsum('hc,kqh->kcq', spread, pair_rows(bb_r), precision=hp)
    bt_i = jnp.einsum('hc,kqh->kcq', spread, pair_rows(bb_i), precision=hp)
    lin_r, lin_i = _cmul(*_a_power(pr[:, None, :], pi[:, None, :], -lane_lag[None, :, None]),
                         bt_r * own.T, bt_i * own.T)

    wk = (jnp.einsum('kcq,kqe->kce', lin_r, rout_r, precision=hp)
          - jnp.einsum('kcq,kqe->kce', lin_i, rout_i, precision=hp)) * causal
    ws_r, ws_i = _cmul(*_a_power(pr[:, None, :], pi[:, None, :], LAG - 1), lin_r, lin_i)
    wo_r, wo_i = _cmul(*_a_power(pr[..., None], pi[..., None], 1), rout_r, rout_i)

    ws = jnp.concatenate([ws_r, ws_i], axis=2)
    wko = jnp.concatenate([wk, wo_r, -wo_i], axis=1)
    return ws.astype(jnp.bfloat16), wko.astype(jnp.bfloat16)


def _chunk_prefix_matrix():
    r = jnp.arange(CHUNK_BLOCK + HALO_ROWS)[:, None]
    c = jnp.arange(CHUNK_BLOCK)[None, :]
    return ((c < r) & (r <= CHUNK_BLOCK)).astype(jnp.bfloat16)


def kernel(x, norm_mix, norm_ffn, norm_final, s5_a_re, s5_a_im, s5_log_dt, s5_b_re, s5_b_im,
           s5_c_re, s5_c_im, s5_d, s5_w_glu, sc_w_in, sc_conv_w, sc_w_out,
           ffn_w_up, ffn_conv_w, ffn_conv_b, ffn_w_down):
    bf16 = jnp.bfloat16
    row = lambda v: v.reshape(1, -1)

    rate_r, rate_i, bb_r, bb_i = _s5_discretise(s5_a_re[0], s5_a_im[0], s5_log_dt[0],
                                                 s5_b_re[0], s5_b_im[0])
    tabs = _s5_tables(rate_r, rate_i)
    ws, wko = _s5_weights(rate_r, rate_i, bb_r, bb_i, s5_c_re[0], s5_c_im[0])
    h, (wup0, wdown0) = _s5_layer(
        x, row(norm_mix[0]), tabs, _chunk_prefix_matrix(), ws, wko, row(s5_d[0]),
        s5_w_glu[0].astype(bf16), tile_rows=1024, cast_next=((ffn_w_up, 0), (ffn_w_down, 0)))
    h, (win, wout) = _ffn_layer(
        h, row(norm_ffn[0]), wup0, ffn_conv_w[0], row(ffn_conv_b[0]), wdown0, row(norm_final),
        tile_rows=1024, final_norm=False, cast_next=((sc_w_in, 0), (sc_w_out, 0)))
    h, (wup1, wdown1) = _shortconv_layer(
        h, row(norm_mix[1]), win, sc_conv_w[0], wout, tile_rows=1024,
        cast_next=((ffn_w_up, 1), (ffn_w_down, 1)))
    h, _ = _ffn_layer(
        h, row(norm_ffn[1]), wup1, ffn_conv_w[1], row(ffn_conv_b[1]), wdown1, row(norm_final),
        tile_rows=1024, final_norm=True)
    return h
```

```python
import functools

import jax
import jax.numpy as jnp
from jax import lax
from jax.experimental import pallas as pl
from jax.experimental.pallas import tpu as pltpu

RMS_EPS = 1e-6
S5_GROUP = 16
S5_STATE = 64
CONV_WIDTH = 3
HALO_ROWS = 8

LANES = 128
BF16_SUBLANES = 16
LAG = 8
PAIR_CH = 32
CHUNK_BLOCK = 16
COL_BLOCK = 256
ROW_BLOCK = 512
S5_TILE_ROWS = 1024
VMEM_LIMIT_BYTES = 56 * 1024 * 1024


def _rmsnorm(x, gain):
    ms = jnp.mean(x * x, axis=-1, keepdims=True)
    return x * lax.rsqrt(ms + RMS_EPS) * gain


def _pack_rows(v):
    return pltpu.bitcast(v.astype(jnp.bfloat16), jnp.uint32)


def _const_spec(shape):
    nd = len(shape)
    return pl.BlockSpec(shape, lambda b, i: (0,) * nd, pipeline_mode=pl.Buffered(1))


def _tile_spec(tile_rows, d):
    return pl.BlockSpec((None, tile_rows, d), lambda b, i: (b, i, 0))


def _compiler_params():
    return pltpu.CompilerParams(
        dimension_semantics=("arbitrary", "arbitrary"),
        vmem_limit_bytes=VMEM_LIMIT_BYTES,
    )


def _cast_blocks(n_rows, n_steps):
    for n_blocks in range(n_steps, 0, -1):
        if (n_steps % n_blocks == 0 and n_rows % n_blocks == 0
                and (n_rows // n_blocks) % BF16_SUBLANES == 0):
            return n_blocks
    raise ValueError(f"no row blocking of {n_rows} rows over {n_steps} steps")


def _sublayer_call(body, name, h, consts, scratch_shapes, *, tile_rows, cast_next=()):
    bsz, seq, d = h.shape
    n_tiles = seq // tile_rows
    n_steps = bsz * n_tiles
    n_in = 1 + len(consts)
    n_cast = len(cast_next)

    def cast_specs(w, layer):
        _, n_rows, n_cols = w.shape
        n_blocks = _cast_blocks(n_rows, n_steps)
        steps_per_block = n_steps // n_blocks
        block = lambda b, i: (b * n_tiles + i) // steps_per_block
        return (pl.BlockSpec((None, n_rows // n_blocks, n_cols), lambda b, i: (layer, block(b, i), 0)),
                pl.BlockSpec((n_rows // n_blocks, n_cols), lambda b, i: (block(b, i), 0)))

    def kernel(*refs):
        ins, cast_ins = refs[:n_in], refs[n_in:n_in + n_cast]
        out, cast_outs = refs[n_in + n_cast], refs[n_in + n_cast + 1:n_in + 2 * n_cast + 1]
        scratch = refs[n_in + 2 * n_cast + 1:]
        for src, dst in zip(cast_ins, cast_outs):
            dst[...] = src[...].astype(dst.dtype)
        body(*ins, out, *scratch)

    cast_in, cast_out = zip(*[cast_specs(w, layer) for w, layer in cast_next]) if cast_next else ((), ())
    outs = pl.pallas_call(
        kernel,
        grid=(bsz, n_tiles),
        in_specs=[_tile_spec(tile_rows, d)] + [_const_spec(c.shape) for c in consts] + list(cast_in),
        out_specs=[_tile_spec(tile_rows, d)] + list(cast_out),
        out_shape=[jax.ShapeDtypeStruct(h.shape, h.dtype)]
        + [jax.ShapeDtypeStruct(w.shape[1:], jnp.bfloat16) for w, _ in cast_next],
        scratch_shapes=scratch_shapes,
        compiler_params=_compiler_params(),
        name=name,
    )(h, *consts, *[w for w, _ in cast_next])
    return outs[0], tuple(outs[1:])


def _s5_kernel(h_ref, gain_ref, er_ref, ei_ref, fr_ref, fi_ref, pr_ref, pi_ref,
               tri_ref, ws_ref, wko_ref, dskip_ref, wglu_ref, o_ref,
               carry_ref, slab_ref, hp_ref):
    @pl.when(pl.program_id(1) == 0)
    def _():
        carry_ref[...] = jnp.zeros_like(carry_ref)

    bf16 = jnp.bfloat16
    f32 = jnp.float32
    tile_rows, d = h_ref.shape
    n_lane_tiles = d // LANES
    pairs_per_lane_tile = LANES // PAIR_CH
    n_pairs = d // PAIR_CH
    pair_modes = (PAIR_CH // S5_GROUP) * S5_STATE
    block_rows = tile_rows
    n_chunks = block_rows // LAG
    n_blocks = tile_rows // block_rows
    tri = tri_ref[...]
    er, ei, fr, fi = er_ref[...], ei_ref[...], fr_ref[...], fi_ref[...]
    pr, pi = pr_ref[...], pi_ref[...]
    state = [dict() for _ in range(n_blocks)]

    def gather_inputs(rb):
        st, r0 = state[rb], rb * block_rows
        st['x'] = h_ref[r0:r0 + block_rows, :]
        st['u'] = _rmsnorm(st['x'], gain_ref[...])
        for j in range(n_lane_tiles):
            slab_ref[j, r0:r0 + block_rows, :] = st['u'][:, j * LANES:(j + 1) * LANES]
        lag_rows = [[_pack_rows(slab_ref[j, pl.ds(r0 + dd, n_chunks, stride=LAG), :])
                     for j in range(n_lane_tiles)] for dd in range(LAG)]
        u2, s_re, s_im = [], [], []
        for k in range(n_pairs):
            j, q = divmod(k, pairs_per_lane_tile)
            lanes = slice(q * PAIR_CH, (q + 1) * PAIR_CH)
            u2k = pltpu.bitcast(
                jnp.concatenate([lag_rows[dd][j][:, lanes] for dd in range(LAG)], axis=1), bf16)
            u2.append(u2k)
            sk = jnp.dot(u2k, ws_ref[k], preferred_element_type=f32)
            s_re.append(sk[:, :pair_modes])
            s_im.append(sk[:, pair_modes:])
        st['u2'] = u2
        st['s_re'] = jnp.concatenate(s_re, axis=1)
        st['s_im'] = jnp.concatenate(s_im, axis=1)

    def chunk_recurrence(rb):
        st, c0 = state[rb], rb * n_chunks
        n_cblk = n_chunks // CHUNK_BLOCK
        cb_rows = CHUNK_BLOCK + HALO_ROWS
        sr = st['s_re'].astype(bf16)
        si = st['s_im'].astype(bf16)
        xr = jnp.dot(tri, er * sr - ei * si, preferred_element_type=f32)
        xi = jnp.dot(tri, er * si + ei * sr, preferred_element_type=f32)
        c_r, c_i = carry_ref[0:1, :], carry_ref[1:2, :]
        cs_r, cs_i = [], []
        for blk in range(n_cblk):
            cs_r.append(c_r)
            cs_i.append(c_i)
            tot = blk * cb_rows + CHUNK_BLOCK
            t_r = xr[tot:tot + 1, :] + c_r
            t_i = xi[tot:tot + 1, :] + c_i
            c_r, c_i = pr * t_r - pi * t_i, pr * t_i + pi * t_r
        carry_ref[0:1, :] = c_r
        carry_ref[1:2, :] = c_i

        def with_carry(x, cs):
            return jnp.concatenate(
                [x[blk * cb_rows:blk * cb_rows + CHUNK_BLOCK, :] + cs[blk] for blk in range(n_cblk)],
                axis=0).astype(bf16)

        gr, gi = with_carry(xr, cs_r), with_carry(xi, cs_i)
        hpr = fr * gr - fi * gi
        hpi = fr * gi + fi * gr
        for k in range(n_pairs):
            md = slice(k * pair_modes, (k + 1) * pair_modes)
            hp_ref[c0:c0 + n_chunks, 2 * k * pair_modes:(2 * k + 1) * pair_modes] = hpr[:, md]
            hp_ref[c0:c0 + n_chunks, (2 * k + 1) * pair_modes:(2 * k + 2) * pair_modes] = hpi[:, md]

    def scatter_outputs(rb):
        st, r0, c0 = state[rb], rb * block_rows, rb * n_chunks
        y2 = []
        for k in range(n_pairs):
            hp = hp_ref[c0:c0 + n_chunks, 2 * k * pair_modes:(2 * k + 2) * pair_modes]
            y2.append(_pack_rows(jnp.dot(jnp.concatenate([st['u2'][k], hp], axis=1), wko_ref[k],
                                         preferred_element_type=f32)))
        for dd in range(LAG):
            lanes = slice(dd * PAIR_CH, (dd + 1) * PAIR_CH)
            for j in range(n_lane_tiles):
                piece = jnp.concatenate(
                    [y2[j * pairs_per_lane_tile + q][:, lanes] for q in range(pairs_per_lane_tile)],
                    axis=1)
                slab_ref[j, pl.ds(r0 + dd, n_chunks, stride=LAG), :] = (
                    pltpu.bitcast(piece, bf16).astype(f32))

    def gated_output(rb):
        st, r0 = state[rb], rb * block_rows
        y = jnp.concatenate([slab_ref[j, r0:r0 + block_rows, :] for j in range(n_lane_tiles)], axis=1)
        y = y + dskip_ref[...] * st['u']
        z = jax.nn.gelu(y).astype(bf16)
        for c in range(d // COL_BLOCK):
            cols = slice(c * COL_BLOCK, (c + 1) * COL_BLOCK)
            gcols = slice(d + c * COL_BLOCK, d + (c + 1) * COL_BLOCK)
            za = jnp.dot(z, wglu_ref[:, cols], preferred_element_type=f32)
            zg = jnp.dot(z, wglu_ref[:, gcols], preferred_element_type=f32)
            o_ref[r0:r0 + block_rows, cols] = st['x'][:, cols] + za * jax.nn.sigmoid(zg)

    for rb in range(n_blocks):
        gather_inputs(rb)
        chunk_recurrence(rb)
        scatter_outputs(rb)
        gated_output(rb)


def _s5_layer(h, gain, tabs, tri, ws, wko, dskip, wglu, *, tile_rows, cast_next):
    d = h.shape[-1]
    n_modes = tabs[0].shape[1]
    scratch = [
        pltpu.VMEM((HALO_ROWS, n_modes), jnp.float32),
        pltpu.VMEM((d // LANES, tile_rows, LANES), jnp.float32),
        pltpu.VMEM((tile_rows // LAG, 2 * n_modes), jnp.bfloat16),
    ]
    return _sublayer_call(_s5_kernel, "s5_mixer", h, (gain, *tabs, tri, ws, wko, dskip, wglu),
                          scratch, tile_rows=tile_rows, cast_next=cast_next)


def _causal_conv_rows(val, halo_ref, w_ref):
    tile_rows = val.shape[0]
    ext = jnp.concatenate([halo_ref[...], val], axis=0)
    out = val * w_ref[CONV_WIDTH - 1:CONV_WIDTH, :]
    for k in range(CONV_WIDTH - 1):
        shift = CONV_WIDTH - 1 - k
        out = out + ext[HALO_ROWS - shift:HALO_ROWS - shift + tile_rows, :] * w_ref[k:k + 1, :]
    halo_ref[...] = val[tile_rows - HALO_ROWS:, :]
    return out


def _zero_halo_at_sequence_start(halo_ref):
    @pl.when(pl.program_id(1) == 0)
    def _():
        halo_ref[...] = jnp.zeros_like(halo_ref)


def _ffn_kernel(h_ref, gain_ref, wup_ref, cw_ref, cb_ref, wdown_ref, fgain_ref, o_ref,
                halo_ref, *, d_ff, final_norm):
    _zero_halo_at_sequence_start(halo_ref)
    for r in range(h_ref.shape[0] // ROW_BLOCK):
        rows = slice(r * ROW_BLOCK, (r + 1) * ROW_BLOCK)
        x = h_ref[rows, :]
        u = _rmsnorm(x, gain_ref[...]).astype(jnp.bfloat16)
        g = jnp.dot(u, wup_ref[:, :d_ff], preferred_element_type=jnp.float32)
        v = jnp.dot(u, wup_ref[:, d_ff:], preferred_element_type=jnp.float32)
        g = _causal_conv_rows(g, halo_ref, cw_ref) + cb_ref[...]
        act = (jax.nn.silu(g) * v).astype(jnp.bfloat16)
        out = x + jnp.dot(act, wdown_ref[...], preferred_element_type=jnp.float32)
        if final_norm:
            out = _rmsnorm(out, fgain_ref[...])
        o_ref[rows, :] = out


def _ffn_layer(h, gain, wup, cw, cb, wdown, fgain, *, tile_rows, final_norm, cast_next=()):
    d_ff = wdown.shape[0]
    body = functools.partial(_ffn_kernel, d_ff=d_ff, final_norm=final_norm)
    scratch = [pltpu.VMEM((HALO_ROWS, d_ff), jnp.float32)]
    return _sublayer_call(body, "conv_ffn_final" if final_norm else "conv_ffn", h,
                          (gain, wup, cw, cb, wdown, fgain), scratch,
                          tile_rows=tile_rows, cast_next=cast_next)


def _shortconv_kernel(h_ref, gain_ref, win_ref, cw_ref, wout_ref, o_ref, halo_ref):
    _zero_halo_at_sequence_start(halo_ref)
    d = h_ref.shape[-1]
    for r in range(h_ref.shape[0] // ROW_BLOCK):
        rows = slice(r * ROW_BLOCK, (r + 1) * ROW_BLOCK)
        x = h_ref[rows, :]
        u = _rmsnorm(x, gain_ref[...]).astype(jnp.bfloat16)
        b_gate, c_gate, hh = [
            jnp.dot(u, win_ref[:, k * d:(k + 1) * d], preferred_element_type=jnp.float32)
            for k in range(3)]
        v = _causal_conv_rows(c_gate * hh, halo_ref, cw_ref)
        o_ref[rows, :] = x + jnp.dot((b_gate * v).astype(jnp.bfloat16), wout_ref[...],
                                     preferred_element_type=jnp.float32)


def _shortconv_layer(h, gain, win, cw, wout, *, tile_rows, cast_next):
    scratch = [pltpu.VMEM((HALO_ROWS, h.shape[-1]), jnp.float32)]
    return _sublayer_call(_shortconv_kernel, "shortconv_mixer", h, (gain, win, cw, wout), scratch,
                          tile_rows=tile_rows, cast_next=cast_next)


def _s5_discretise(a_re, a_im, log_dt, b_re, b_im):
    f32 = jnp.float32
    lam_r = a_re.astype(f32)
    lam_i = a_im.astype(f32)
    dt = jnp.exp(log_dt.astype(f32))[:, None]
    mag = jnp.exp(lam_r * dt)
    ab_r = mag * jnp.cos(lam_i * dt)
    ab_i = mag * jnp.sin(lam_i * dt)
    den = lam_r * lam_r + lam_i * lam_i
    nr = ab_r - 1.0
    g_r = ((nr * lam_r + ab_i * lam_i) / den)[..., None]
    g_i = ((ab_i * lam_r - nr * lam_i) / den)[..., None]
    br = b_re.astype(f32)
    bi = b_im.astype(f32)
    return lam_r * dt, lam_i * dt, g_r * br - g_i * bi, g_r * bi + g_i * br


def _a_power(rate_r, rate_i, k):
    k = jnp.asarray(k, jnp.float32)
    mag = jnp.exp(rate_r * k)
    return mag * jnp.cos(rate_i * k), mag * jnp.sin(rate_i * k)


def _cmul(ar, ai, br, bi):
    return ar * br - ai * bi, ar * bi + ai * br


def _s5_tables(rate_r, rate_i, n_chunk_blocks):
    rr = rate_r.reshape(1, -1)
    ri = rate_i.reshape(1, -1)
    mid = CHUNK_BLOCK // 2
    c = jnp.arange(CHUNK_BLOCK)[:, None]
    er, ei = _a_power(rr, ri, LAG * (mid - c))
    fr, fi = _a_power(rr, ri, LAG * (c - 1 - mid))
    pr, pi = _a_power(rr, ri, LAG * CHUNK_BLOCK)
    rep = lambda t: jnp.tile(t.astype(jnp.bfloat16), (n_chunk_blocks, 1))
    return rep(er), rep(ei), rep(fr), rep(fi), pr, pi


def _s5_weights(rate_r, rate_i, bb_r, bb_i, c_re, c_im):
    f32 = jnp.float32
    hp = lax.Precision.HIGHEST
    n_groups, n_state, n_ch = bb_r.shape
    gpp = PAIR_CH // n_ch
    n_pairs = n_groups // gpp
    wide = LAG * PAIR_CH
    pair_modes = gpp * n_state
    lane = jnp.arange(wide)
    lane_lag, lane_g2, lane_ch = lane // PAIR_CH, (lane // n_ch) % gpp, lane % n_ch
    mode_g2 = jnp.arange(pair_modes) // n_state
    ch = jnp.arange(n_ch)
    lags = jnp.arange(LAG)
    spread = (ch[:, None] == lane_ch[None, :]).astype(f32)
    own = (mode_g2[:, None] == lane_g2[None, :]).astype(f32)
    causal = (lane_lag[None, :] >= lane_lag[:, None]).astype(f32)
    pr = rate_r.reshape(n_pairs, pair_modes)
    pi = rate_i.reshape(n_pairs, pair_modes)

    def pair_rows(w):
        return w.reshape((n_pairs, pair_modes) + w.shape[2:])

    cw_r = jnp.einsum('gop,oc->gpc', c_re.astype(f32), spread, precision=hp)
    cw_i = jnp.einsum('gop,oc->gpc', c_im.astype(f32), spread, precision=hp)
    lag_spread = (lags[:, None] == lane_lag[None, :]).astype(f32)
    ad_r, ad_i = [jnp.einsum('kqd,dc->kqc', t, lag_spread, precision=hp)
                  for t in _a_power(pr[..., None], pi[..., None], lags)]
    rout_r, rout_i = _cmul(ad_r, ad_i, pair_rows(cw_r) * own, pair_rows(cw_i) * own)
    bt_r = jnp.einsum('hc,kqh->kcq', spread, pair_rows(bb_r), precision=hp)
    bt_i = jnp.einsum('hc,kqh->kcq', spread, pair_rows(bb_i), precision=hp)
    an_r, an_i = [jnp.repeat(t, PAIR_CH, axis=1)
                  for t in _a_power(pr[:, None, :], pi[:, None, :], -lags[None, :, None])]
    lin_r, lin_i = _cmul(an_r, an_i, bt_r * own.T, bt_i * own.T)

    wk = (jnp.einsum('kcq,kqe->kce', lin_r, rout_r, precision=hp)
          - jnp.einsum('kcq,kqe->kce', lin_i, rout_i, precision=hp)) * causal
    ws_r, ws_i = _cmul(*_a_power(pr[:, None, :], pi[:, None, :], LAG - 1), lin_r, lin_i)
    wo_r, wo_i = _cmul(*_a_power(pr[..., None], pi[..., None], 1), rout_r, rout_i)

    ws = jnp.concatenate([ws_r, ws_i], axis=2)
    wko = jnp.concatenate([wk, wo_r, -wo_i], axis=1)
    return ws.astype(jnp.bfloat16), wko.astype(jnp.bfloat16)


def _chunk_prefix_matrix(n_chunk_blocks):
    r = jnp.arange(CHUNK_BLOCK + HALO_ROWS)[:, None]
    c = jnp.arange(CHUNK_BLOCK)[None, :]
    one_block = ((c < r) & (r <= CHUNK_BLOCK)).astype(jnp.float32)
    return jnp.kron(jnp.eye(n_chunk_blocks, dtype=jnp.float32), one_block).astype(jnp.bfloat16)


def kernel(x, norm_mix, norm_ffn, norm_final, s5_a_re, s5_a_im, s5_log_dt, s5_b_re, s5_b_im,
           s5_c_re, s5_c_im, s5_d, s5_w_glu, sc_w_in, sc_conv_w, sc_w_out,
           ffn_w_up, ffn_conv_w, ffn_conv_b, ffn_w_down):
    bf16 = jnp.bfloat16
    row = lambda v: v.reshape(1, -1)

    rate_r, rate_i, bb_r, bb_i = _s5_discretise(s5_a_re[0], s5_a_im[0], s5_log_dt[0],
                                                 s5_b_re[0], s5_b_im[0])
    n_chunk_blocks = S5_TILE_ROWS // (LAG * CHUNK_BLOCK)
    tabs = _s5_tables(rate_r, rate_i, n_chunk_blocks)
    ws, wko = _s5_weights(rate_r, rate_i, bb_r, bb_i, s5_c_re[0], s5_c_im[0])
    h, (wup0, wdown0) = _s5_layer(
        x, row(norm_mix[0]), tabs, _chunk_prefix_matrix(n_chunk_blocks), ws, wko, row(s5_d[0]),
        s5_w_glu[0].astype(bf16), tile_rows=S5_TILE_ROWS,
        cast_next=((ffn_w_up, 0), (ffn_w_down, 0)))
    h, (win, wout) = _ffn_layer(
        h, row(norm_ffn[0]), wup0, ffn_conv_w[0], row(ffn_conv_b[0]), wdown0, row(norm_final),
        tile_rows=1024, final_norm=False, cast_next=((sc_w_in, 0), (sc_w_out, 0)))
    h, (wup1, wdown1) = _shortconv_layer(
        h, row(norm_mix[1]), win, sc_conv_w[0], wout, tile_rows=1024,
        cast_next=((ffn_w_up, 1), (ffn_w_down, 1)))
    h, _ = _ffn_layer(
        h, row(norm_ffn[1]), wup1, ffn_conv_w[1], row(ffn_conv_b[1]), wdown1, row(norm_final),
        tile_rows=1024, final_norm=True)
    return h
```

```python
import functools

import jax
import jax.numpy as jnp
from jax import lax
from jax.experimental import pallas as pl
from jax.experimental.pallas import tpu as pltpu

RMS_EPS = 1e-6
S5_GROUP = 16
S5_STATE = 64
CONV_WIDTH = 3
HALO_ROWS = 8

LANES = 128
BF16_SUBLANES = 16
LAG = 8
PAIR_CH = 32
CHUNK_BLOCK = 16
COL_BLOCK = 256
ROW_BLOCK = 512
S5_TILE_ROWS = 1024
VMEM_LIMIT_BYTES = 56 * 1024 * 1024


def _rmsnorm(x, gain):
    ms = jnp.mean(x * x, axis=-1, keepdims=True)
    return x * lax.rsqrt(ms + RMS_EPS) * gain


def _pack_rows(v):
    return pltpu.bitcast(v.astype(jnp.bfloat16), jnp.uint32)


def _const_spec(shape):
    nd = len(shape)
    return pl.BlockSpec(shape, lambda b, i: (0,) * nd, pipeline_mode=pl.Buffered(1))


def _tile_spec(tile_rows, d):
    return pl.BlockSpec((None, tile_rows, d), lambda b, i: (b, i, 0))


def _compiler_params():
    return pltpu.CompilerParams(
        dimension_semantics=("arbitrary", "arbitrary"),
        vmem_limit_bytes=VMEM_LIMIT_BYTES,
    )


def _cast_blocks(n_rows, n_steps):
    for n_blocks in range(n_steps, 0, -1):
        if (n_steps % n_blocks == 0 and n_rows % n_blocks == 0
                and (n_rows // n_blocks) % BF16_SUBLANES == 0):
            return n_blocks
    raise ValueError(f"no row blocking of {n_rows} rows over {n_steps} steps")


def _sublayer_call(body, name, h, consts, scratch_shapes, *, tile_rows, cast_next=()):
    bsz, seq, d = h.shape
    n_tiles = seq // tile_rows
    n_steps = bsz * n_tiles
    n_in = 1 + len(consts)
    n_cast = len(cast_next)

    def cast_specs(w, layer):
        _, n_rows, n_cols = w.shape
        n_blocks = _cast_blocks(n_rows, n_steps)
        steps_per_block = n_steps // n_blocks
        block = lambda b, i: (b * n_tiles + i) // steps_per_block
        return (pl.BlockSpec((None, n_rows // n_blocks, n_cols), lambda b, i: (layer, block(b, i), 0)),
                pl.BlockSpec((n_rows // n_blocks, n_cols), lambda b, i: (block(b, i), 0)))

    def kernel(*refs):
        ins, cast_ins = refs[:n_in], refs[n_in:n_in + n_cast]
        out, cast_outs = refs[n_in + n_cast], refs[n_in + n_cast + 1:n_in + 2 * n_cast + 1]
        scratch = refs[n_in + 2 * n_cast + 1:]
        for src, dst in zip(cast_ins, cast_outs):
            dst[...] = src[...].astype(dst.dtype)
        body(*ins, out, *scratch)

    cast_in, cast_out = zip(*[cast_specs(w, layer) for w, layer in cast_next]) if cast_next else ((), ())
    outs = pl.pallas_call(
        kernel,
        grid=(bsz, n_tiles),
        in_specs=[_tile_spec(tile_rows, d)] + [_const_spec(c.shape) for c in consts] + list(cast_in),
        out_specs=[_tile_spec(tile_rows, d)] + list(cast_out),
        out_shape=[jax.ShapeDtypeStruct(h.shape, h.dtype)]
        + [jax.ShapeDtypeStruct(w.shape[1:], jnp.bfloat16) for w, _ in cast_next],
        scratch_shapes=scratch_shapes,
        compiler_params=_compiler_params(),
        name=name,
    )(h, *consts, *[w for w, _ in cast_next])
    return outs[0], tuple(outs[1:])


def _s5_kernel(h_ref, gain_ref, er_ref, ei_ref, fr_ref, fi_ref, pr_ref, pi_ref,
               tri_ref, ws_ref, wko_ref, dskip_ref, wglu_ref, o_ref,
               carry_ref, slab_ref, hp_ref):
    @pl.when(pl.program_id(1) == 0)
    def _():
        carry_ref[...] = jnp.zeros_like(carry_ref)

    bf16 = jnp.bfloat16
    f32 = jnp.float32
    tile_rows, d = h_ref.shape
    n_lane_tiles = d // LANES
    pairs_per_lane_tile = LANES // PAIR_CH
    n_pairs = d // PAIR_CH
    pair_modes = (PAIR_CH // S5_GROUP) * S5_STATE
    block_rows = tile_rows
    n_chunks = block_rows // LAG
    n_blocks = tile_rows // block_rows
    tri = tri_ref[...]
    er, ei, fr, fi = er_ref[...], ei_ref[...], fr_ref[...], fi_ref[...]
    pr, pi = pr_ref[...], pi_ref[...]
    state = [dict() for _ in range(n_blocks)]

    def gather_inputs(rb):
        st, r0 = state[rb], rb * block_rows
        st['x'] = h_ref[r0:r0 + block_rows, :]
        st['u'] = _rmsnorm(st['x'], gain_ref[...])
        for j in range(n_lane_tiles):
            slab_ref[j, r0:r0 + block_rows, :] = st['u'][:, j * LANES:(j + 1) * LANES]
        lag_rows = [[_pack_rows(slab_ref[j, pl.ds(r0 + dd, n_chunks, stride=LAG), :])
                     for j in range(n_lane_tiles)] for dd in range(LAG)]
        u2, s_re, s_im = [], [], []
        for k in range(n_pairs):
            j, q = divmod(k, pairs_per_lane_tile)
            lanes = slice(q * PAIR_CH, (q + 1) * PAIR_CH)
            u2k = pltpu.bitcast(
                jnp.concatenate([lag_rows[dd][j][:, lanes] for dd in range(LAG)], axis=1), bf16)
            u2.append(u2k)
            sk = jnp.dot(u2k, ws_ref[k], preferred_element_type=f32)
            s_re.append(sk[:, :pair_modes])
            s_im.append(sk[:, pair_modes:])
        st['u2'] = u2
        st['s_re'] = jnp.concatenate(s_re, axis=1)
        st['s_im'] = jnp.concatenate(s_im, axis=1)

    def chunk_recurrence(rb):
        st, c0 = state[rb], rb * n_chunks
        n_cblk = n_chunks // CHUNK_BLOCK
        cb_rows = CHUNK_BLOCK + HALO_ROWS
        sr = st['s_re'].astype(bf16)
        si = st['s_im'].astype(bf16)
        xr = jnp.dot(tri, er * sr - ei * si, preferred_element_type=f32)
        xi = jnp.dot(tri, er * si + ei * sr, preferred_element_type=f32)
        c_r, c_i = carry_ref[0:1, :], carry_ref[1:2, :]
        cs_r, cs_i = [], []
        for blk in range(n_cblk):
            cs_r.append(c_r)
            cs_i.append(c_i)
            tot = blk * cb_rows + CHUNK_BLOCK
            t_r = xr[tot:tot + 1, :] + c_r
            t_i = xi[tot:tot + 1, :] + c_i
            c_r, c_i = pr * t_r - pi * t_i, pr * t_i + pi * t_r
        carry_ref[0:1, :] = c_r
        carry_ref[1:2, :] = c_i

        def with_carry(x, cs):
            return jnp.concatenate(
                [x[blk * cb_rows:blk * cb_rows + CHUNK_BLOCK, :] + cs[blk] for blk in range(n_cblk)],
                axis=0).astype(bf16)

        gr, gi = with_carry(xr, cs_r), with_carry(xi, cs_i)
        hpr = fr * gr - fi * gi
        hpi = fr * gi + fi * gr
        for k in range(n_pairs):
            md = slice(k * pair_modes, (k + 1) * pair_modes)
            hp_ref[c0:c0 + n_chunks, 2 * k * pair_modes:(2 * k + 1) * pair_modes] = hpr[:, md]
            hp_ref[c0:c0 + n_chunks, (2 * k + 1) * pair_modes:(2 * k + 2) * pair_modes] = hpi[:, md]

    def scatter_outputs(rb):
        st, r0, c0 = state[rb], rb * block_rows, rb * n_chunks
        y2 = []
        for k in range(n_pairs):
            hp = hp_ref[c0:c0 + n_chunks, 2 * k * pair_modes:(2 * k + 2) * pair_modes]
            y2.append(_pack_rows(jnp.dot(jnp.concatenate([st['u2'][k], hp], axis=1), wko_ref[k],
                                         preferred_element_type=f32)))
        for dd in range(LAG):
            lanes = slice(dd * PAIR_CH, (dd + 1) * PAIR_CH)
            for j in range(n_lane_tiles):
                piece = jnp.concatenate(
                    [y2[j * pairs_per_lane_tile + q][:, lanes] for q in range(pairs_per_lane_tile)],
                    axis=1)
                slab_ref[j, pl.ds(r0 + dd, n_chunks, stride=LAG), :] = (
                    pltpu.bitcast(piece, bf16).astype(f32))

    def gated_output(rb):
        st, r0 = state[rb], rb * block_rows
        y = jnp.concatenate([slab_ref[j, r0:r0 + block_rows, :] for j in range(n_lane_tiles)], axis=1)
        y = y + dskip_ref[...] * st['u']
        z = jax.nn.gelu(y).astype(bf16)
        for c in range(d // COL_BLOCK):
            cols = slice(c * COL_BLOCK, (c + 1) * COL_BLOCK)
            gcols = slice(d + c * COL_BLOCK, d + (c + 1) * COL_BLOCK)
            za = jnp.dot(z, wglu_ref[:, cols], preferred_element_type=f32)
            zg = jnp.dot(z, wglu_ref[:, gcols], preferred_element_type=f32)
            o_ref[r0:r0 + block_rows, cols] = st['x'][:, cols] + za * jax.nn.sigmoid(zg)

    for rb in range(n_blocks):
        gather_inputs(rb)
        chunk_recurrence(rb)
        scatter_outputs(rb)
        gated_output(rb)


def _s5_layer(h, gain, tabs, tri, ws, wko, dskip, wglu, *, tile_rows, cast_next):
    d = h.shape[-1]
    n_modes = tabs[0].shape[1]
    scratch = [
        pltpu.VMEM((HALO_ROWS, n_modes), jnp.float32),
        pltpu.VMEM((d // LANES, tile_rows, LANES), jnp.float32),
        pltpu.VMEM((tile_rows // LAG, 2 * n_modes), jnp.bfloat16),
    ]
    return _sublayer_call(_s5_kernel, "s5_mixer", h, (gain, *tabs, tri, ws, wko, dskip, wglu),
                          scratch, tile_rows=tile_rows, cast_next=cast_next)


def _causal_conv_rows(val, halo_ref, w_ref, cols=slice(None)):
    tile_rows = val.shape[0]
    ext = jnp.concatenate([halo_ref[:, cols], val], axis=0)
    out = val * w_ref[CONV_WIDTH - 1:CONV_WIDTH, cols]
    for k in range(CONV_WIDTH - 1):
        shift = CONV_WIDTH - 1 - k
        out = out + ext[HALO_ROWS - shift:HALO_ROWS - shift + tile_rows, :] * w_ref[k:k + 1, cols]
    halo_ref[:, cols] = val[tile_rows - HALO_ROWS:, :]
    return out


def _zero_halo_at_sequence_start(halo_ref):
    @pl.when(pl.program_id(1) == 0)
    def _():
        halo_ref[...] = jnp.zeros_like(halo_ref)


def _ffn_kernel(h_ref, gain_ref, wup_ref, cw_ref, cb_ref, wdown_ref, fgain_ref, o_ref,
                halo_ref, *, d_ff, final_norm):
    _zero_halo_at_sequence_start(halo_ref)
    for r in range(h_ref.shape[0] // ROW_BLOCK):
        rows = slice(r * ROW_BLOCK, (r + 1) * ROW_BLOCK)
        x = h_ref[rows, :]
        u = _rmsnorm(x, gain_ref[...]).astype(jnp.bfloat16)
        g = jnp.dot(u, wup_ref[:, :d_ff], preferred_element_type=jnp.float32)
        v = jnp.dot(u, wup_ref[:, d_ff:], preferred_element_type=jnp.float32)
        g = _causal_conv_rows(g, halo_ref, cw_ref) + cb_ref[...]
        act = (jax.nn.silu(g) * v).astype(jnp.bfloat16)
        out = x + jnp.dot(act, wdown_ref[...], preferred_element_type=jnp.float32)
        if final_norm:
            out = _rmsnorm(out, fgain_ref[...])
        o_ref[rows, :] = out


def _ffn_layer(h, gain, wup, cw, cb, wdown, fgain, *, tile_rows, final_norm, cast_next=()):
    d_ff = wdown.shape[0]
    body = functools.partial(_ffn_kernel, d_ff=d_ff, final_norm=final_norm)
    scratch = [pltpu.VMEM((HALO_ROWS, d_ff), jnp.float32)]
    return _sublayer_call(body, "conv_ffn_final" if final_norm else "conv_ffn", h,
                          (gain, wup, cw, cb, wdown, fgain), scratch,
                          tile_rows=tile_rows, cast_next=cast_next)


def _shortconv_kernel(h_ref, gain_ref, win_ref, cw_ref, wout_ref, o_ref, halo_ref):
    _zero_halo_at_sequence_start(halo_ref)
    d = h_ref.shape[-1]
    for r in range(h_ref.shape[0] // ROW_BLOCK):
        rows = slice(r * ROW_BLOCK, (r + 1) * ROW_BLOCK)
        x = h_ref[rows, :]
        u = _rmsnorm(x, gain_ref[...]).astype(jnp.bfloat16)
        gated = []
        for c in range(d // COL_BLOCK):
            cols = slice(c * COL_BLOCK, (c + 1) * COL_BLOCK)
            b_gate, c_gate, hh = [
                jnp.dot(u, win_ref[:, k * d + c * COL_BLOCK:k * d + (c + 1) * COL_BLOCK],
                        preferred_element_type=jnp.float32) for k in range(3)]
            v = _causal_conv_rows(c_gate * hh, halo_ref, cw_ref, cols)
            gated.append((b_gate * v).astype(jnp.bfloat16))
        o_ref[rows, :] = x + jnp.dot(jnp.concatenate(gated, axis=1), wout_ref[...],
                                     preferred_element_type=jnp.float32)


def _shortconv_layer(h, gain, win, cw, wout, *, tile_rows, cast_next):
    scratch = [pltpu.VMEM((HALO_ROWS, h.shape[-1]), jnp.float32)]
    return _sublayer_call(_shortconv_kernel, "shortconv_mixer", h, (gain, win, cw, wout), scratch,
                          tile_rows=tile_rows, cast_next=cast_next)


def _s5_discretise(a_re, a_im, log_dt, b_re, b_im):
    f32 = jnp.float32
    lam_r = a_re.astype(f32)
    lam_i = a_im.astype(f32)
    dt = jnp.exp(log_dt.astype(f32))[:, None]
    mag = jnp.exp(lam_r * dt)
    ab_r = mag * jnp.cos(lam_i * dt)
    ab_i = mag * jnp.sin(lam_i * dt)
    den = lam_r * lam_r + lam_i * lam_i
    nr = ab_r - 1.0
    g_r = ((nr * lam_r + ab_i * lam_i) / den)[..., None]
    g_i = ((ab_i * lam_r - nr * lam_i) / den)[..., None]
    br = b_re.astype(f32)
    bi = b_im.astype(f32)
    return lam_r * dt, lam_i * dt, g_r * br - g_i * bi, g_r * bi + g_i * br


def _a_power(rate_r, rate_i, k):
    k = jnp.asarray(k, jnp.float32)
    mag = jnp.exp(rate_r * k)
    return mag * jnp.cos(rate_i * k), mag * jnp.sin(rate_i * k)


def _cmul(ar, ai, br, bi):
    return ar * br - ai * bi, ar * bi + ai * br


def _s5_tables(rate_r, rate_i, n_chunk_blocks):
    rr = rate_r.reshape(1, -1)
    ri = rate_i.reshape(1, -1)
    mid = CHUNK_BLOCK // 2
    c = jnp.arange(CHUNK_BLOCK)[:, None]
    er, ei = _a_power(rr, ri, LAG * (mid - c))
    fr, fi = _a_power(rr, ri, LAG * (c - 1 - mid))
    pr, pi = _a_power(rr, ri, LAG * CHUNK_BLOCK)
    rep = lambda t: jnp.tile(t.astype(jnp.bfloat16), (n_chunk_blocks, 1))
    return rep(er), rep(ei), rep(fr), rep(fi), pr, pi


def _s5_weights(rate_r, rate_i, bb_r, bb_i, c_re, c_im):
    f32 = jnp.float32
    hp = lax.Precision.HIGHEST
    n_groups, n_state, n_ch = bb_r.shape
    gpp = PAIR_CH // n_ch
    n_pairs = n_groups // gpp
    wide = LAG * PAIR_CH
    pair_modes = gpp * n_state
    lane = jnp.arange(wide)
    lane_lag, lane_g2, lane_ch = lane // PAIR_CH, (lane // n_ch) % gpp, lane % n_ch
    mode_g2 = jnp.arange(pair_modes) // n_state
    ch = jnp.arange(n_ch)
    lags = jnp.arange(LAG)
    spread = (ch[:, None] == lane_ch[None, :]).astype(f32)
    own = (mode_g2[:, None] == lane_g2[None, :]).astype(f32)
    causal = (lane_lag[None, :] >= lane_lag[:, None]).astype(f32)
    pr = rate_r.reshape(n_pairs, pair_modes)
    pi = rate_i.reshape(n_pairs, pair_modes)

    def pair_rows(w):
        return w.reshape((n_pairs, pair_modes) + w.shape[2:])

    cw_r = jnp.einsum('gop,oc->gpc', c_re.astype(f32), spread, precision=hp)
    cw_i = jnp.einsum('gop,oc->gpc', c_im.astype(f32), spread, precision=hp)
    lag_spread = (lags[:, None] == lane_lag[None, :]).astype(f32)
    ad_r, ad_i = [jnp.einsum('kqd,dc->kqc', t, lag_spread, precision=hp)
                  for t in _a_power(pr[..., None], pi[..., None], lags)]
    rout_r, rout_i = _cmul(ad_r, ad_i, pair_rows(cw_r) * own, pair_rows(cw_i) * own)
    bt_r = jnp.einsum('hc,kqh->kcq', spread, pair_rows(bb_r), precision=hp)
    bt_i = jnp.einsum('hc,kqh->kcq', spread, pair_rows(bb_i), precision=hp)
    an_r, an_i = [jnp.repeat(t, PAIR_CH, axis=1)
                  for t in _a_power(pr[:, None, :], pi[:, None, :], -lags[None, :, None])]
    lin_r, lin_i = _cmul(an_r, an_i, bt_r * own.T, bt_i * own.T)

    wk = (jnp.einsum('kcq,kqe->kce', lin_r, rout_r, precision=hp)
          - jnp.einsum('kcq,kqe->kce', lin_i, rout_i, precision=hp)) * causal
    ws_r, ws_i = _cmul(*_a_power(pr[:, None, :], pi[:, None, :], LAG - 1), lin_r, lin_i)
    wo_r, wo_i = _cmul(*_a_power(pr[..., None], pi[..., None], 1), rout_r, rout_i)

    ws = jnp.concatenate([ws_r, ws_i], axis=2)
    wko = jnp.concatenate([wk, wo_r, -wo_i], axis=1)
    return ws.astype(jnp.bfloat16), wko.astype(jnp.bfloat16)


def _chunk_prefix_matrix(n_chunk_blocks):
    r = jnp.arange(CHUNK_BLOCK + HALO_ROWS)[:, None]
    c = jnp.arange(CHUNK_BLOCK)[None, :]
    one_block = ((c < r) & (r <= CHUNK_BLOCK)).astype(jnp.float32)
    return jnp.kron(jnp.eye(n_chunk_blocks, dtype=jnp.float32), one_block).astype(jnp.bfloat16)


def kernel(x, norm_mix, norm_ffn, norm_final, s5_a_re, s5_a_im, s5_log_dt, s5_b_re, s5_b_im,
           s5_c_re, s5_c_im, s5_d, s5_w_glu, sc_w_in, sc_conv_w, sc_w_out,
           ffn_w_up, ffn_conv_w, ffn_conv_b, ffn_w_down):
    bf16 = jnp.bfloat16
    row = lambda v: v.reshape(1, -1)

    rate_r, rate_i, bb_r, bb_i = _s5_discretise(s5_a_re[0], s5_a_im[0], s5_log_dt[0],
                                                 s5_b_re[0], s5_b_im[0])
    n_chunk_blocks = S5_TILE_ROWS // (LAG * CHUNK_BLOCK)
    tabs = _s5_tables(rate_r, rate_i, n_chunk_blocks)
    ws, wko = _s5_weights(rate_r, rate_i, bb_r, bb_i, s5_c_re[0], s5_c_im[0])
    h, (wup0, wdown0) = _s5_layer(
        x, row(norm_mix[0]), tabs, _chunk_prefix_matrix(n_chunk_blocks), ws, wko, row(s5_d[0]),
        s5_w_glu[0].astype(bf16), tile_rows=S5_TILE_ROWS,
        cast_next=((ffn_w_up, 0), (ffn_w_down, 0)))
    h, (win, wout) = _ffn_layer(
        h, row(norm_ffn[0]), wup0, ffn_conv_w[0], row(ffn_conv_b[0]), wdown0, row(norm_final),
        tile_rows=1024, final_norm=False, cast_next=((sc_w_in, 0), (sc_w_out, 0)))
    h, (wup1, wdown1) = _shortconv_layer(
        h, row(norm_mix[1]), win, sc_conv_w[0], wout, tile_rows=1024,
        cast_next=((ffn_w_up, 1), (ffn_w_down, 1)))
    h, _ = _ffn_layer(
        h, row(norm_ffn[1]), wup1, ffn_conv_w[1], row(ffn_conv_b[1]), wdown1, row(norm_final),
        tile_rows=1024, final_norm=True)
    return h
```

```python
import functools

import jax
import jax.numpy as jnp
from jax import lax
from jax.experimental import pallas as pl
from jax.experimental.pallas import tpu as pltpu

RMS_EPS = 1e-6
S5_GROUP = 16
S5_STATE = 64
CONV_WIDTH = 3
HALO_ROWS = 8

LANES = 128
BF16_SUBLANES = 16
LAG = 8
PAIR_CH = 32
CHUNK_BLOCK = 16
COL_BLOCK = 256
ROW_BLOCK = 512
TILE_ROWS = 1024
VMEM_LIMIT_BYTES = 56 * 1024 * 1024


def _rmsnorm(x, gain):
    ms = jnp.mean(x * x, axis=-1, keepdims=True)
    return x * lax.rsqrt(ms + RMS_EPS) * gain


def _pack_rows(v):
    return pltpu.bitcast(v.astype(jnp.bfloat16), jnp.uint32)


def _const_spec(shape):
    nd = len(shape)
    return pl.BlockSpec(shape, lambda b, i: (0,) * nd, pipeline_mode=pl.Buffered(1))


def _tile_spec(tile_rows, d):
    return pl.BlockSpec((None, tile_rows, d), lambda b, i: (b, i, 0))


def _compiler_params():
    return pltpu.CompilerParams(
        dimension_semantics=("arbitrary", "arbitrary"),
        vmem_limit_bytes=VMEM_LIMIT_BYTES,
    )


def _cast_blocks(n_rows, n_steps):
    for n_blocks in range(n_steps, 0, -1):
        if (n_steps % n_blocks == 0 and n_rows % n_blocks == 0
                and (n_rows // n_blocks) % BF16_SUBLANES == 0):
            return n_blocks
    raise ValueError(f"no row blocking of {n_rows} rows over {n_steps} steps")


def _sublayer_call(body, name, h, consts, scratch_shapes, *, tile_rows, cast_next=()):
    bsz, seq, d = h.shape
    n_tiles = seq // tile_rows
    n_steps = bsz * n_tiles
    n_in = 1 + len(consts)
    n_cast = len(cast_next)

    def cast_specs(w, layer):
        _, n_rows, n_cols = w.shape
        n_blocks = _cast_blocks(n_rows, n_steps)
        steps_per_block = n_steps // n_blocks
        block = lambda b, i: (b * n_tiles + i) // steps_per_block
        return (pl.BlockSpec((None, n_rows // n_blocks, n_cols), lambda b, i: (layer, block(b, i), 0)),
                pl.BlockSpec((n_rows // n_blocks, n_cols), lambda b, i: (block(b, i), 0)))

    def kernel(*refs):
        ins, cast_ins = refs[:n_in], refs[n_in:n_in + n_cast]
        out, cast_outs = refs[n_in + n_cast], refs[n_in + n_cast + 1:n_in + 2 * n_cast + 1]
        scratch = refs[n_in + 2 * n_cast + 1:]
        for src, dst in zip(cast_ins, cast_outs):
            dst[...] = src[...].astype(dst.dtype)
        body(*ins, out, *scratch)

    cast_in, cast_out = zip(*[cast_specs(w, layer) for w, layer in cast_next]) if cast_next else ((), ())
    outs = pl.pallas_call(
        kernel,
        grid=(bsz, n_tiles),
        in_specs=[_tile_spec(tile_rows, d)] + [_const_spec(c.shape) for c in consts] + list(cast_in),
        out_specs=[_tile_spec(tile_rows, d)] + list(cast_out),
        out_shape=[jax.ShapeDtypeStruct(h.shape, h.dtype)]
        + [jax.ShapeDtypeStruct(w.shape[1:], jnp.bfloat16) for w, _ in cast_next],
        scratch_shapes=scratch_shapes,
        compiler_params=_compiler_params(),
        name=name,
    )(h, *consts, *[w for w, _ in cast_next])
    return outs[0], tuple(outs[1:])


def _s5_kernel(h_ref, gain_ref, er_ref, ei_ref, fr_ref, fi_ref, pr_ref, pi_ref,
               tri_ref, ws_ref, wko_ref, dskip_ref, wglu_ref, o_ref,
               carry_ref, slab_ref, hp_ref):
    @pl.when(pl.program_id(1) == 0)
    def _():
        carry_ref[...] = jnp.zeros_like(carry_ref)

    bf16 = jnp.bfloat16
    f32 = jnp.float32
    tile_rows, d = h_ref.shape
    n_lane_tiles = d // LANES
    pairs_per_lane_tile = LANES // PAIR_CH
    n_pairs = d // PAIR_CH
    pair_modes = (PAIR_CH // S5_GROUP) * S5_STATE
    n_chunks = tile_rows // LAG
    n_cblk = n_chunks // CHUNK_BLOCK
    cb_rows = CHUNK_BLOCK + HALO_ROWS

    x = h_ref[...]
    u = _rmsnorm(x, gain_ref[...])
    for j in range(n_lane_tiles):
        slab_ref[j] = u[:, j * LANES:(j + 1) * LANES]
    lag_rows = [[_pack_rows(slab_ref[j, pl.ds(dd, n_chunks, stride=LAG), :])
                 for j in range(n_lane_tiles)] for dd in range(LAG)]
    u2, s_re, s_im = [], [], []
    for k in range(n_pairs):
        j, q = divmod(k, pairs_per_lane_tile)
        lanes = slice(q * PAIR_CH, (q + 1) * PAIR_CH)
        u2k = pltpu.bitcast(
            jnp.concatenate([lag_rows[dd][j][:, lanes] for dd in range(LAG)], axis=1), bf16)
        u2.append(u2k)
        sk = jnp.dot(u2k, ws_ref[k], preferred_element_type=f32)
        s_re.append(sk[:, :pair_modes])
        s_im.append(sk[:, pair_modes:])
    sr = jnp.concatenate(s_re, axis=1).astype(bf16)
    si = jnp.concatenate(s_im, axis=1).astype(bf16)

    er, ei, fr, fi = er_ref[...], ei_ref[...], fr_ref[...], fi_ref[...]
    pr, pi = pr_ref[...], pi_ref[...]
    xr = jnp.dot(tri_ref[...], er * sr - ei * si, preferred_element_type=f32)
    xi = jnp.dot(tri_ref[...], er * si + ei * sr, preferred_element_type=f32)
    c_r, c_i = carry_ref[0:1, :], carry_ref[1:2, :]
    cs_r, cs_i = [], []
    for blk in range(n_cblk):
        cs_r.append(c_r)
        cs_i.append(c_i)
        tot = blk * cb_rows + CHUNK_BLOCK
        t_r = xr[tot:tot + 1, :] + c_r
        t_i = xi[tot:tot + 1, :] + c_i
        c_r, c_i = pr * t_r - pi * t_i, pr * t_i + pi * t_r
    carry_ref[0:1, :] = c_r
    carry_ref[1:2, :] = c_i

    def with_carry(v, cs):
        return jnp.concatenate(
            [v[blk * cb_rows:blk * cb_rows + CHUNK_BLOCK, :] + cs[blk] for blk in range(n_cblk)],
            axis=0).astype(bf16)

    gr, gi = with_carry(xr, cs_r), with_carry(xi, cs_i)
    hpr = fr * gr - fi * gi
    hpi = fr * gi + fi * gr
    for k in range(n_pairs):
        md = slice(k * pair_modes, (k + 1) * pair_modes)
        hp_ref[:, 2 * k * pair_modes:(2 * k + 1) * pair_modes] = hpr[:, md]
        hp_ref[:, (2 * k + 1) * pair_modes:(2 * k + 2) * pair_modes] = hpi[:, md]

    y2 = []
    for k in range(n_pairs):
        hp = hp_ref[:, 2 * k * pair_modes:(2 * k + 2) * pair_modes]
        y2.append(_pack_rows(jnp.dot(jnp.concatenate([u2[k], hp], axis=1), wko_ref[k],
                                     preferred_element_type=f32)))
    for dd in range(LAG):
        lanes = slice(dd * PAIR_CH, (dd + 1) * PAIR_CH)
        for j in range(n_lane_tiles):
            piece = jnp.concatenate(
                [y2[j * pairs_per_lane_tile + q][:, lanes] for q in range(pairs_per_lane_tile)],
                axis=1)
            slab_ref[j, pl.ds(dd, n_chunks, stride=LAG), :] = pltpu.bitcast(piece, bf16).astype(f32)
    y = jnp.concatenate([slab_ref[j] for j in range(n_lane_tiles)], axis=1)

    z = jax.nn.gelu(y + dskip_ref[...] * u).astype(bf16)
    for c in range(d // COL_BLOCK):
        cols = slice(c * COL_BLOCK, (c + 1) * COL_BLOCK)
        gcols = slice(d + c * COL_BLOCK, d + (c + 1) * COL_BLOCK)
        za = jnp.dot(z, wglu_ref[:, cols], preferred_element_type=f32)
        zg = jnp.dot(z, wglu_ref[:, gcols], preferred_element_type=f32)
        o_ref[:, cols] = x[:, cols] + za * jax.nn.sigmoid(zg)


def _s5_layer(h, gain, tabs, tri, ws, wko, dskip, wglu, *, tile_rows, cast_next):
    d = h.shape[-1]
    n_modes = tabs[0].shape[1]
    scratch = [
        pltpu.VMEM((HALO_ROWS, n_modes), jnp.float32),
        pltpu.VMEM((d // LANES, tile_rows, LANES), jnp.float32),
        pltpu.VMEM((tile_rows // LAG, 2 * n_modes), jnp.bfloat16),
    ]
    return _sublayer_call(_s5_kernel, "s5_mixer", h, (gain, *tabs, tri, ws, wko, dskip, wglu),
                          scratch, tile_rows=tile_rows, cast_next=cast_next)


def _causal_conv_rows(val, halo_ref, w_ref, cols=slice(None)):
    tile_rows = val.shape[0]
    ext = jnp.concatenate([halo_ref[:, cols], val], axis=0)
    out = val * w_ref[CONV_WIDTH - 1:CONV_WIDTH, cols]
    for k in range(CONV_WIDTH - 1):
        shift = CONV_WIDTH - 1 - k
        out = out + ext[HALO_ROWS - shift:HALO_ROWS - shift + tile_rows, :] * w_ref[k:k + 1, cols]
    halo_ref[:, cols] = val[tile_rows - HALO_ROWS:, :]
    return out


def _zero_halo_at_sequence_start(halo_ref):
    @pl.when(pl.program_id(1) == 0)
    def _():
        halo_ref[...] = jnp.zeros_like(halo_ref)


def _ffn_kernel(h_ref, gain_ref, wup_ref, cw_ref, cb_ref, wdown_ref, fgain_ref, o_ref,
                halo_ref, *, d_ff, final_norm):
    _zero_halo_at_sequence_start(halo_ref)
    for r in range(h_ref.shape[0] // ROW_BLOCK):
        rows = slice(r * ROW_BLOCK, (r + 1) * ROW_BLOCK)
        x = h_ref[rows, :]
        u = _rmsnorm(x, gain_ref[...]).astype(jnp.bfloat16)
        g = jnp.dot(u, wup_ref[:, :d_ff], preferred_element_type=jnp.float32)
        v = jnp.dot(u, wup_ref[:, d_ff:], preferred_element_type=jnp.float32)
        g = _causal_conv_rows(g, halo_ref, cw_ref) + cb_ref[...]
        act = (jax.nn.silu(g) * v).astype(jnp.bfloat16)
        out = x + jnp.dot(act, wdown_ref[...], preferred_element_type=jnp.float32)
        if final_norm:
            out = _rmsnorm(out, fgain_ref[...])
        o_ref[rows, :] = out


def _ffn_layer(h, gain, wup, cw, cb, wdown, fgain, *, tile_rows, final_norm, cast_next=()):
    d_ff = wdown.shape[0]
    body = functools.partial(_ffn_kernel, d_ff=d_ff, final_norm=final_norm)
    scratch = [pltpu.VMEM((HALO_ROWS, d_ff), jnp.float32)]
    return _sublayer_call(body, "conv_ffn_final" if final_norm else "conv_ffn", h,
                          (gain, wup, cw, cb, wdown, fgain), scratch,
                          tile_rows=tile_rows, cast_next=cast_next)


def _shortconv_kernel(h_ref, gain_ref, win_ref, cw_ref, wout_ref, o_ref, halo_ref):
    _zero_halo_at_sequence_start(halo_ref)
    d = h_ref.shape[-1]
    for r in range(h_ref.shape[0] // ROW_BLOCK):
        rows = slice(r * ROW_BLOCK, (r + 1) * ROW_BLOCK)
        x = h_ref[rows, :]
        u = _rmsnorm(x, gain_ref[...]).astype(jnp.bfloat16)
        gated = []
        for c in range(d // COL_BLOCK):
            cols = slice(c * COL_BLOCK, (c + 1) * COL_BLOCK)
            b_gate, c_gate, hh = [
                jnp.dot(u, win_ref[:, k * d + c * COL_BLOCK:k * d + (c + 1) * COL_BLOCK],
                        preferred_element_type=jnp.float32) for k in range(3)]
            v = _causal_conv_rows(c_gate * hh, halo_ref, cw_ref, cols)
            gated.append((b_gate * v).astype(jnp.bfloat16))
        o_ref[rows, :] = x + jnp.dot(jnp.concatenate(gated, axis=1), wout_ref[...],
                                     preferred_element_type=jnp.float32)


def _shortconv_layer(h, gain, win, cw, wout, *, tile_rows, cast_next):
    scratch = [pltpu.VMEM((HALO_ROWS, h.shape[-1]), jnp.float32)]
    return _sublayer_call(_shortconv_kernel, "shortconv_mixer", h, (gain, win, cw, wout), scratch,
                          tile_rows=tile_rows, cast_next=cast_next)


def _s5_discretise(a_re, a_im, log_dt, b_re, b_im):
    f32 = jnp.float32
    lam_r = a_re.astype(f32)
    lam_i = a_im.astype(f32)
    dt = jnp.exp(log_dt.astype(f32))[:, None]
    mag = jnp.exp(lam_r * dt)
    ab_r = mag * jnp.cos(lam_i * dt)
    ab_i = mag * jnp.sin(lam_i * dt)
    den = lam_r * lam_r + lam_i * lam_i
    nr = ab_r - 1.0
    g_r = ((nr * lam_r + ab_i * lam_i) / den)[..., None]
    g_i = ((ab_i * lam_r - nr * lam_i) / den)[..., None]
    br = b_re.astype(f32)
    bi = b_im.astype(f32)
    return lam_r * dt, lam_i * dt, g_r * br - g_i * bi, g_r * bi + g_i * br


def _cmul(ar, ai, br, bi):
    return ar * br - ai * bi, ar * bi + ai * br


def _s5_powers(rate_r, rate_i, n_chunk_blocks):
    mid = CHUNK_BLOCK // 2
    c = jnp.arange(CHUNK_BLOCK)
    lags = jnp.arange(LAG)
    groups = [LAG * (mid - c), LAG * (c - 1 - mid), jnp.array([LAG * CHUNK_BLOCK]), lags, -lags]
    k = jnp.concatenate(groups).astype(jnp.float32)[:, None]
    rr = rate_r.reshape(1, -1)
    ri = rate_i.reshape(1, -1)
    mag = jnp.exp(rr * k)
    re, im = mag * jnp.cos(ri * k), mag * jnp.sin(ri * k)
    bounds = [0]
    for g in groups:
        bounds.append(bounds[-1] + g.shape[0])
    part = lambda t, n: t[bounds[n]:bounds[n + 1]]
    rep = lambda t: jnp.tile(t.astype(jnp.bfloat16), (n_chunk_blocks, 1))
    tables = (rep(part(re, 0)), rep(part(im, 0)), rep(part(re, 1)), rep(part(im, 1)),
              part(re, 2), part(im, 2))
    return tables, (part(re, 3), part(im, 3), part(re, 4), part(im, 4))


def _s5_weights(lag_powers, bb_r, bb_i, c_re, c_im):
    f32 = jnp.float32
    hp = lax.Precision.HIGHEST
    n_groups, n_state, n_ch = bb_r.shape
    gpp = PAIR_CH // n_ch
    n_pairs = n_groups // gpp
    wide = LAG * PAIR_CH
    pair_modes = gpp * n_state
    lane = jnp.arange(wide)
    lane_lag, lane_g2, lane_ch = lane // PAIR_CH, (lane // n_ch) % gpp, lane % n_ch
    mode_g2 = jnp.arange(pair_modes) // n_state
    ch = jnp.arange(n_ch)
    lags = jnp.arange(LAG)
    spread = (ch[:, None] == lane_ch[None, :]).astype(f32)
    lag_spread = (lags[:, None] == lane_lag[None, :]).astype(f32)
    own = (mode_g2[:, None] == lane_g2[None, :]).astype(f32)
    causal = (lane_lag[None, :] >= lane_lag[:, None]).astype(f32)
    by_pair = lambda t: t.reshape(LAG, n_pairs, pair_modes)
    ap_r, ap_i, an_r, an_i = map(by_pair, lag_powers)

    def pair_rows(w):
        return w.reshape((n_pairs, pair_modes) + w.shape[2:])

    cw_r = jnp.einsum('gop,oc->gpc', c_re.astype(f32), spread, precision=hp)
    cw_i = jnp.einsum('gop,oc->gpc', c_im.astype(f32), spread, precision=hp)
    ad_r = jnp.einsum('dkq,dc->kqc', ap_r, lag_spread, precision=hp)
    ad_i = jnp.einsum('dkq,dc->kqc', ap_i, lag_spread, precision=hp)
    rout_r, rout_i = _cmul(ad_r, ad_i, pair_rows(cw_r) * own, pair_rows(cw_i) * own)
    bt_r = jnp.einsum('hc,kqh->kcq', spread, pair_rows(bb_r), precision=hp)
    bt_i = jnp.einsum('hc,kqh->kcq', spread, pair_rows(bb_i), precision=hp)
    rows_of_lag = lambda t: jnp.repeat(jnp.transpose(t, (1, 0, 2)), PAIR_CH, axis=1)
    lin_r, lin_i = _cmul(rows_of_lag(an_r), rows_of_lag(an_i), bt_r * own.T, bt_i * own.T)

    wk = (jnp.einsum('kcq,kqe->kce', lin_r, rout_r, precision=hp)
          - jnp.einsum('kcq,kqe->kce', lin_i, rout_i, precision=hp)) * causal
    ws_r, ws_i = _cmul(ap_r[LAG - 1][:, None, :], ap_i[LAG - 1][:, None, :], lin_r, lin_i)
    wo_r, wo_i = _cmul(ap_r[1][:, :, None], ap_i[1][:, :, None], rout_r, rout_i)

    ws = jnp.concatenate([ws_r, ws_i], axis=2)
    wko = jnp.concatenate([wk, wo_r, -wo_i], axis=1)
    return ws.astype(jnp.bfloat16), wko.astype(jnp.bfloat16)


def _chunk_prefix_matrix(n_chunk_blocks):
    r = jnp.arange(CHUNK_BLOCK + HALO_ROWS)[:, None]
    c = jnp.arange(CHUNK_BLOCK)[None, :]
    one_block = ((c < r) & (r <= CHUNK_BLOCK)).astype(jnp.float32)
    return jnp.kron(jnp.eye(n_chunk_blocks, dtype=jnp.float32), one_block).astype(jnp.bfloat16)


def kernel(x, norm_mix, norm_ffn, norm_final, s5_a_re, s5_a_im, s5_log_dt, s5_b_re, s5_b_im,
           s5_c_re, s5_c_im, s5_d, s5_w_glu, sc_w_in, sc_conv_w, sc_w_out,
           ffn_w_up, ffn_conv_w, ffn_conv_b, ffn_w_down):
    bf16 = jnp.bfloat16
    row = lambda v: v.reshape(1, -1)

    rate_r, rate_i, bb_r, bb_i = _s5_discretise(s5_a_re[0], s5_a_im[0], s5_log_dt[0],
                                                 s5_b_re[0], s5_b_im[0])
    n_chunk_blocks = TILE_ROWS // (LAG * CHUNK_BLOCK)
    tabs, lag_powers = _s5_powers(rate_r, rate_i, n_chunk_blocks)
    ws, wko = _s5_weights(lag_powers, bb_r, bb_i, s5_c_re[0], s5_c_im[0])
    h, (wup0, wdown0) = _s5_layer(
        x, row(norm_mix[0]), tabs, _chunk_prefix_matrix(n_chunk_blocks), ws, wko, row(s5_d[0]),
        s5_w_glu[0].astype(bf16), tile_rows=TILE_ROWS,
        cast_next=((ffn_w_up, 0), (ffn_w_down, 0)))
    h, (win, wout) = _ffn_layer(
        h, row(norm_ffn[0]), wup0, ffn_conv_w[0], row(ffn_conv_b[0]), wdown0, row(norm_final),
        tile_rows=TILE_ROWS, final_norm=False, cast_next=((sc_w_in, 0), (sc_w_out, 0)))
    h, (wup1, wdown1) = _shortconv_layer(
        h, row(norm_mix[1]), win, sc_conv_w[0], wout, tile_rows=TILE_ROWS,
        cast_next=((ffn_w_up, 1), (ffn_w_down, 1)))
    h, _ = _ffn_layer(
        h, row(norm_ffn[1]), wup1, ffn_conv_w[1], row(ffn_conv_b[1]), wdown1, row(norm_final),
        tile_rows=TILE_ROWS, final_norm=True)
    return h
```

```python
import functools

import jax
import jax.numpy as jnp
from jax import lax
from jax.experimental import pallas as pl
from jax.experimental.pallas import tpu as pltpu

RMS_EPS = 1e-6
S5_GROUP = 16
S5_STATE = 64
CONV_WIDTH = 3
HALO_ROWS = 8

LANES = 128
BF16_SUBLANES = 16
LAG = 8
PAIR_CH = 32
CHUNK_BLOCK = 16
COL_BLOCK = 256
ROW_BLOCK = 512
TILE_ROWS = 1024
S5_ROW_PARTS = 2
VMEM_LIMIT_BYTES = 56 * 1024 * 1024


def _rmsnorm(x, gain):
    ms = jnp.mean(x * x, axis=-1, keepdims=True)
    return x * lax.rsqrt(ms + RMS_EPS) * gain


def _pack_rows(v):
    return pltpu.bitcast(v.astype(jnp.bfloat16), jnp.uint32)


def _const_spec(shape):
    nd = len(shape)
    return pl.BlockSpec(shape, lambda b, i: (0,) * nd, pipeline_mode=pl.Buffered(1))


def _tile_spec(tile_rows, d):
    return pl.BlockSpec((None, tile_rows, d), lambda b, i: (b, i, 0))


def _compiler_params():
    return pltpu.CompilerParams(
        dimension_semantics=("arbitrary", "arbitrary"),
        vmem_limit_bytes=VMEM_LIMIT_BYTES,
    )


def _cast_blocks(n_rows, n_steps):
    for n_blocks in range(n_steps, 0, -1):
        if (n_steps % n_blocks == 0 and n_rows % n_blocks == 0
                and (n_rows // n_blocks) % BF16_SUBLANES == 0):
            return n_blocks
    raise ValueError(f"no row blocking of {n_rows} rows over {n_steps} steps")


def _sublayer_call(body, name, h, consts, scratch_shapes, *, tile_rows, cast_next=()):
    bsz, seq, d = h.shape
    n_tiles = seq // tile_rows
    n_steps = bsz * n_tiles
    n_in = 1 + len(consts)
    n_cast = len(cast_next)

    def cast_specs(w, layer):
        _, n_rows, n_cols = w.shape
        n_blocks = _cast_blocks(n_rows, n_steps)
        steps_per_block = n_steps // n_blocks
        block = lambda b, i: (b * n_tiles + i) // steps_per_block
        return (pl.BlockSpec((None, n_rows // n_blocks, n_cols), lambda b, i: (layer, block(b, i), 0)),
                pl.BlockSpec((n_rows // n_blocks, n_cols), lambda b, i: (block(b, i), 0)))

    def kernel(*refs):
        ins, cast_ins = refs[:n_in], refs[n_in:n_in + n_cast]
        out, cast_outs = refs[n_in + n_cast], refs[n_in + n_cast + 1:n_in + 2 * n_cast + 1]
        scratch = refs[n_in + 2 * n_cast + 1:]
        for src, dst in zip(cast_ins, cast_outs):
            dst[...] = src[...].astype(dst.dtype)
        body(*ins, out, *scratch)

    cast_in, cast_out = zip(*[cast_specs(w, layer) for w, layer in cast_next]) if cast_next else ((), ())
    outs = pl.pallas_call(
        kernel,
        grid=(bsz, n_tiles),
        in_specs=[_tile_spec(tile_rows, d)] + [_const_spec(c.shape) for c in consts] + list(cast_in),
        out_specs=[_tile_spec(tile_rows, d)] + list(cast_out),
        out_shape=[jax.ShapeDtypeStruct(h.shape, h.dtype)]
        + [jax.ShapeDtypeStruct(w.shape[1:], jnp.bfloat16) for w, _ in cast_next],
        scratch_shapes=scratch_shapes,
        compiler_params=_compiler_params(),
        name=name,
    )(h, *consts, *[w for w, _ in cast_next])
    return outs[0], tuple(outs[1:])


def _s5_kernel(h_ref, gain_ref, er_ref, ei_ref, fr_ref, fi_ref, pr_ref, pi_ref,
               tri_ref, ws_ref, wko_ref, dskip_ref, wglu_ref, o_ref,
               carry_ref, slab_ref, hp_ref):
    @pl.when(pl.program_id(1) == 0)
    def _():
        carry_ref[...] = jnp.zeros_like(carry_ref)

    bf16 = jnp.bfloat16
    f32 = jnp.float32
    tile_rows, d = h_ref.shape
    n_lane_tiles = d // LANES
    pairs_per_lane_tile = LANES // PAIR_CH
    n_pairs = d // PAIR_CH
    pair_modes = (PAIR_CH // S5_GROUP) * S5_STATE
    n_chunks = tile_rows // LAG
    n_cblk = n_chunks // CHUNK_BLOCK
    cb_rows = CHUNK_BLOCK + HALO_ROWS

    x = h_ref[...]
    u = _rmsnorm(x, gain_ref[...])
    part_rows = tile_rows // S5_ROW_PARTS
    part_chunks = n_chunks // S5_ROW_PARTS
    for hf in range(S5_ROW_PARTS):
        for j in range(n_lane_tiles):
            slab_ref[hf, j] = u[hf * part_rows:(hf + 1) * part_rows, j * LANES:(j + 1) * LANES]
    lag_rows = [[_pack_rows(jnp.concatenate(
        [slab_ref[hf, j, pl.ds(dd, part_chunks, stride=LAG), :] for hf in range(S5_ROW_PARTS)], axis=0))
        for j in range(n_lane_tiles)] for dd in range(LAG)]
    u2, s_re, s_im = [], [], []
    for k in range(n_pairs):
        j, q = divmod(k, pairs_per_lane_tile)
        lanes = slice(q * PAIR_CH, (q + 1) * PAIR_CH)
        u2k = pltpu.bitcast(
            jnp.concatenate([lag_rows[dd][j][:, lanes] for dd in range(LAG)], axis=1), bf16)
        u2.append(u2k)
        sk = jnp.dot(u2k, ws_ref[k], preferred_element_type=f32)
        s_re.append(sk[:, :pair_modes])
        s_im.append(sk[:, pair_modes:])
    sr = jnp.concatenate(s_re, axis=1).astype(bf16)
    si = jnp.concatenate(s_im, axis=1).astype(bf16)

    er, ei, fr, fi = er_ref[...], ei_ref[...], fr_ref[...], fi_ref[...]
    pr, pi = pr_ref[...], pi_ref[...]
    xr = jnp.dot(tri_ref[...], er * sr - ei * si, preferred_element_type=f32)
    xi = jnp.dot(tri_ref[...], er * si + ei * sr, preferred_element_type=f32)
    c_r, c_i = carry_ref[0:1, :], carry_ref[1:2, :]
    cs_r, cs_i = [], []
    for blk in range(n_cblk):
        cs_r.append(c_r)
        cs_i.append(c_i)
        tot = blk * cb_rows + CHUNK_BLOCK
        t_r = xr[tot:tot + 1, :] + c_r
        t_i = xi[tot:tot + 1, :] + c_i
        c_r, c_i = pr * t_r - pi * t_i, pr * t_i + pi * t_r
    carry_ref[0:1, :] = c_r
    carry_ref[1:2, :] = c_i

    def with_carry(v, cs):
        return jnp.concatenate(
            [v[blk * cb_rows:blk * cb_rows + CHUNK_BLOCK, :] + cs[blk] for blk in range(n_cblk)],
            axis=0).astype(bf16)

    gr, gi = with_carry(xr, cs_r), with_carry(xi, cs_i)
    hpr = fr * gr - fi * gi
    hpi = fr * gi + fi * gr
    for k in range(n_pairs):
        md = slice(k * pair_modes, (k + 1) * pair_modes)
        hp_ref[:, 2 * k * pair_modes:(2 * k + 1) * pair_modes] = hpr[:, md]
        hp_ref[:, (2 * k + 1) * pair_modes:(2 * k + 2) * pair_modes] = hpi[:, md]

    y2 = []
    for k in range(n_pairs):
        hp = hp_ref[:, 2 * k * pair_modes:(2 * k + 2) * pair_modes]
        y2.append(_pack_rows(jnp.dot(jnp.concatenate([u2[k], hp], axis=1), wko_ref[k],
                                     preferred_element_type=f32)))
    for dd in range(LAG):
        lanes = slice(dd * PAIR_CH, (dd + 1) * PAIR_CH)
        for j in range(n_lane_tiles):
            piece = jnp.concatenate(
                [y2[j * pairs_per_lane_tile + q][:, lanes] for q in range(pairs_per_lane_tile)],
                axis=1)
            rows_f32 = pltpu.bitcast(piece, bf16).astype(f32)
            for hf in range(S5_ROW_PARTS):
                slab_ref[hf, j, pl.ds(dd, part_chunks, stride=LAG), :] = (
                    rows_f32[hf * part_chunks:(hf + 1) * part_chunks, :])

    for hf in range(S5_ROW_PARTS):
        rows = slice(hf * part_rows, (hf + 1) * part_rows)
        y = jnp.concatenate([slab_ref[hf, j] for j in range(n_lane_tiles)], axis=1)
        z = jax.nn.gelu(y + dskip_ref[...] * u[rows, :]).astype(bf16)
        for c in range(d // COL_BLOCK):
            cols = slice(c * COL_BLOCK, (c + 1) * COL_BLOCK)
            gcols = slice(d + c * COL_BLOCK, d + (c + 1) * COL_BLOCK)
            za = jnp.dot(z, wglu_ref[:, cols], preferred_element_type=f32)
            zg = jnp.dot(z, wglu_ref[:, gcols], preferred_element_type=f32)
            o_ref[rows, cols] = x[rows, cols] + za * jax.nn.sigmoid(zg)


def _s5_layer(h, gain, tabs, tri, ws, wko, dskip, wglu, *, tile_rows, cast_next):
    d = h.shape[-1]
    n_modes = tabs[0].shape[1]
    scratch = [
        pltpu.VMEM((HALO_ROWS, n_modes), jnp.float32),
        pltpu.VMEM((S5_ROW_PARTS, d // LANES, tile_rows // S5_ROW_PARTS, LANES), jnp.float32),
        pltpu.VMEM((tile_rows // LAG, 2 * n_modes), jnp.bfloat16),
    ]
    return _sublayer_call(_s5_kernel, "s5_mixer", h, (gain, *tabs, tri, ws, wko, dskip, wglu),
                          scratch, tile_rows=tile_rows, cast_next=cast_next)


def _causal_conv_rows(val, halo_ref, w_ref, cols=slice(None)):
    tile_rows = val.shape[0]
    ext = jnp.concatenate([halo_ref[:, cols], val], axis=0)
    out = val * w_ref[CONV_WIDTH - 1:CONV_WIDTH, cols]
    for k in range(CONV_WIDTH - 1):
        shift = CONV_WIDTH - 1 - k
        out = out + ext[HALO_ROWS - shift:HALO_ROWS - shift + tile_rows, :] * w_ref[k:k + 1, cols]
    halo_ref[:, cols] = val[tile_rows - HALO_ROWS:, :]
    return out


def _zero_halo_at_sequence_start(halo_ref):
    @pl.when(pl.program_id(1) == 0)
    def _():
        halo_ref[...] = jnp.zeros_like(halo_ref)


def _ffn_kernel(h_ref, gain_ref, wup_ref, cw_ref, cb_ref, wdown_ref, fgain_ref, o_ref,
                halo_ref, *, d_ff, final_norm):
    _zero_halo_at_sequence_start(halo_ref)
    for r in range(h_ref.shape[0] // ROW_BLOCK):
        rows = slice(r * ROW_BLOCK, (r + 1) * ROW_BLOCK)
        x = h_ref[rows, :]
        u = _rmsnorm(x, gain_ref[...]).astype(jnp.bfloat16)
        g = jnp.dot(u, wup_ref[:, :d_ff], preferred_element_type=jnp.float32)
        v = jnp.dot(u, wup_ref[:, d_ff:], preferred_element_type=jnp.float32)
        g = _causal_conv_rows(g, halo_ref, cw_ref) + cb_ref[...]
        act = (jax.nn.silu(g) * v).astype(jnp.bfloat16)
        out = x + jnp.dot(act, wdown_ref[...], preferred_element_type=jnp.float32)
        if final_norm:
            out = _rmsnorm(out, fgain_ref[...])
        o_ref[rows, :] = out


def _ffn_layer(h, gain, wup, cw, cb, wdown, fgain, *, tile_rows, final_norm, cast_next=()):
    d_ff = wdown.shape[0]
    body = functools.partial(_ffn_kernel, d_ff=d_ff, final_norm=final_norm)
    scratch = [pltpu.VMEM((HALO_ROWS, d_ff), jnp.float32)]
    return _sublayer_call(body, "conv_ffn_final" if final_norm else "conv_ffn", h,
                          (gain, wup, cw, cb, wdown, fgain), scratch,
                          tile_rows=tile_rows, cast_next=cast_next)


def _shortconv_kernel(h_ref, gain_ref, win_ref, cw_ref, wout_ref, o_ref, halo_ref):
    _zero_halo_at_sequence_start(halo_ref)
    d = h_ref.shape[-1]
    for r in range(h_ref.shape[0] // ROW_BLOCK):
        rows = slice(r * ROW_BLOCK, (r + 1) * ROW_BLOCK)
        x = h_ref[rows, :]
        u = _rmsnorm(x, gain_ref[...]).astype(jnp.bfloat16)
        gated = []
        for c in range(d // COL_BLOCK):
            cols = slice(c * COL_BLOCK, (c + 1) * COL_BLOCK)
            b_gate, c_gate, hh = [
                jnp.dot(u, win_ref[:, k * d + c * COL_BLOCK:k * d + (c + 1) * COL_BLOCK],
                        preferred_element_type=jnp.float32) for k in range(3)]
            v = _causal_conv_rows(c_gate * hh, halo_ref, cw_ref, cols)
            gated.append((b_gate * v).astype(jnp.bfloat16))
        o_ref[rows, :] = x + jnp.dot(jnp.concatenate(gated, axis=1), wout_ref[...],
                                     preferred_element_type=jnp.float32)


def _shortconv_layer(h, gain, win, cw, wout, *, tile_rows, cast_next):
    scratch = [pltpu.VMEM((HALO_ROWS, h.shape[-1]), jnp.float32)]
    return _sublayer_call(_shortconv_kernel, "shortconv_mixer", h, (gain, win, cw, wout), scratch,
                          tile_rows=tile_rows, cast_next=cast_next)


def _s5_discretise(a_re, a_im, log_dt, b_re, b_im):
    f32 = jnp.float32
    lam_r = a_re.astype(f32)
    lam_i = a_im.astype(f32)
    dt = jnp.exp(log_dt.astype(f32))[:, None]
    mag = jnp.exp(lam_r * dt)
    ab_r = mag * jnp.cos(lam_i * dt)
    ab_i = mag * jnp.sin(lam_i * dt)
    den = lam_r * lam_r + lam_i * lam_i
    nr = ab_r - 1.0
    g_r = ((nr * lam_r + ab_i * lam_i) / den)[..., None]
    g_i = ((ab_i * lam_r - nr * lam_i) / den)[..., None]
    br = b_re.astype(f32)
    bi = b_im.astype(f32)
    return lam_r * dt, lam_i * dt, g_r * br - g_i * bi, g_r * bi + g_i * br


def _a_power(rate_r, rate_i, k):
    k = jnp.asarray(k, jnp.float32)
    mag = jnp.exp(rate_r * k)
    return mag * jnp.cos(rate_i * k), mag * jnp.sin(rate_i * k)


def _cmul(ar, ai, br, bi):
    return ar * br - ai * bi, ar * bi + ai * br


def _s5_tables(rate_r, rate_i, n_chunk_blocks):
    rr = rate_r.reshape(1, -1)
    ri = rate_i.reshape(1, -1)
    mid = CHUNK_BLOCK // 2
    c = jnp.arange(CHUNK_BLOCK)[:, None]
    er, ei = _a_power(rr, ri, LAG * (mid - c))
    fr, fi = _a_power(rr, ri, LAG * (c - 1 - mid))
    pr, pi = _a_power(rr, ri, LAG * CHUNK_BLOCK)
    rep = lambda t: jnp.tile(t.astype(jnp.bfloat16), (n_chunk_blocks, 1))
    return rep(er), rep(ei), rep(fr), rep(fi), pr, pi


def _s5_weights(rate_r, rate_i, bb_r, bb_i, c_re, c_im):
    f32 = jnp.float32
    hp = lax.Precision.HIGHEST
    n_groups, n_state, n_ch = bb_r.shape
    gpp = PAIR_CH // n_ch
    n_pairs = n_groups // gpp
    wide = LAG * PAIR_CH
    pair_modes = gpp * n_state
    lane = jnp.arange(wide)
    lane_lag, lane_g2, lane_ch = lane // PAIR_CH, (lane // n_ch) % gpp, lane % n_ch
    mode_g2 = jnp.arange(pair_modes) // n_state
    ch = jnp.arange(n_ch)
    lags = jnp.arange(LAG)
    spread = (ch[:, None] == lane_ch[None, :]).astype(f32)
    lag_spread = (lags[:, None] == lane_lag[None, :]).astype(f32)
    own = (mode_g2[:, None] == lane_g2[None, :]).astype(f32)
    causal = (lane_lag[None, :] >= lane_lag[:, None]).astype(f32)
    pr = rate_r.reshape(n_pairs, pair_modes)
    pi = rate_i.reshape(n_pairs, pair_modes)

    def pair_rows(w):
        return w.reshape((n_pairs, pair_modes) + w.shape[2:])

    cw_r = jnp.einsum('gop,oc->gpc', c_re.astype(f32), spread, precision=hp)
    cw_i = jnp.einsum('gop,oc->gpc', c_im.astype(f32), spread, precision=hp)
    ad_r, ad_i = [jnp.einsum('kqd,dc->kqc', t, lag_spread, precision=hp)
                  for t in _a_power(pr[..., None], pi[..., None], lags)]
    rout_r, rout_i = _cmul(ad_r, ad_i, pair_rows(cw_r) * own, pair_rows(cw_i) * own)
    bt_r = jnp.einsum('hc,kqh->kcq', spread, pair_rows(bb_r), precision=hp)
    bt_i = jnp.einsum('hc,kqh->kcq', spread, pair_rows(bb_i), precision=hp)
    an_r, an_i = [jnp.repeat(t, PAIR_CH, axis=1)
                  for t in _a_power(pr[:, None, :], pi[:, None, :], -lags[None, :, None])]
    lin_r, lin_i = _cmul(an_r, an_i, bt_r * own.T, bt_i * own.T)

    wk = (jnp.einsum('kcq,kqe->kce', lin_r, rout_r, precision=hp)
          - jnp.einsum('kcq,kqe->kce', lin_i, rout_i, precision=hp)) * causal
    ws_r, ws_i = _cmul(*_a_power(pr[:, None, :], pi[:, None, :], LAG - 1), lin_r, lin_i)
    wo_r, wo_i = _cmul(*_a_power(pr[..., None], pi[..., None], 1), rout_r, rout_i)

    ws = jnp.concatenate([ws_r, ws_i], axis=2)
    wko = jnp.concatenate([wk, wo_r, -wo_i], axis=1)
    return ws.astype(jnp.bfloat16), wko.astype(jnp.bfloat16)


def _chunk_prefix_matrix(n_chunk_blocks):
    r = jnp.arange(CHUNK_BLOCK + HALO_ROWS)[:, None]
    c = jnp.arange(CHUNK_BLOCK)[None, :]
    one_block = ((c < r) & (r <= CHUNK_BLOCK)).astype(jnp.float32)
    return jnp.kron(jnp.eye(n_chunk_blocks, dtype=jnp.float32), one_block).astype(jnp.bfloat16)


def kernel(x, norm_mix, norm_ffn, norm_final, s5_a_re, s5_a_im, s5_log_dt, s5_b_re, s5_b_im,
           s5_c_re, s5_c_im, s5_d, s5_w_glu, sc_w_in, sc_conv_w, sc_w_out,
           ffn_w_up, ffn_conv_w, ffn_conv_b, ffn_w_down):
    bf16 = jnp.bfloat16
    row = lambda v: v.reshape(1, -1)

    rate_r, rate_i, bb_r, bb_i = _s5_discretise(s5_a_re[0], s5_a_im[0], s5_log_dt[0],
                                                 s5_b_re[0], s5_b_im[0])
    n_chunk_blocks = TILE_ROWS // (LAG * CHUNK_BLOCK)
    tabs = _s5_tables(rate_r, rate_i, n_chunk_blocks)
    ws, wko = _s5_weights(rate_r, rate_i, bb_r, bb_i, s5_c_re[0], s5_c_im[0])
    h, (wup0, wdown0) = _s5_layer(
        x, row(norm_mix[0]), tabs, _chunk_prefix_matrix(n_chunk_blocks), ws, wko, row(s5_d[0]),
        s5_w_glu[0].astype(bf16), tile_rows=TILE_ROWS,
        cast_next=((ffn_w_up, 0), (ffn_w_down, 0)))
    h, (win, wout) = _ffn_layer(
        h, row(norm_ffn[0]), wup0, ffn_conv_w[0], row(ffn_conv_b[0]), wdown0, row(norm_final),
        tile_rows=TILE_ROWS, final_norm=False, cast_next=((sc_w_in, 0), (sc_w_out, 0)))
    h, (wup1, wdown1) = _shortconv_layer(
        h, row(norm_mix[1]), win, sc_conv_w[0], wout, tile_rows=TILE_ROWS,
        cast_next=((ffn_w_up, 1), (ffn_w_down, 1)))
    h, _ = _ffn_layer(
        h, row(norm_ffn[1]), wup1, ffn_conv_w[1], row(ffn_conv_b[1]), wdown1, row(norm_final),
        tile_rows=TILE_ROWS, final_norm=True)
    return h
```

```python
import functools

import jax
import jax.numpy as jnp
from jax import lax
from jax.experimental import pallas as pl
from jax.experimental.pallas import tpu as pltpu

RMS_EPS = 1e-6
S5_GROUP = 16
S5_STATE = 64
CONV_WIDTH = 3
HALO_ROWS = 8

LANES = 128
BF16_SUBLANES = 16
LAG = 8
PAIR_CH = 32
CHUNK_BLOCK = 16
COL_BLOCK = 256
ROW_BLOCK = 512
SHORTCONV_ROW_BLOCK = 1024
TILE_ROWS = 1024
S5_ROW_PARTS = 2
VMEM_LIMIT_BYTES = 56 * 1024 * 1024


def _rmsnorm(x, gain):
    ms = jnp.mean(x * x, axis=-1, keepdims=True)
    return x * lax.rsqrt(ms + RMS_EPS) * gain


def _pack_rows(v):
    return pltpu.bitcast(v.astype(jnp.bfloat16), jnp.uint32)


def _const_spec(shape):
    nd = len(shape)
    return pl.BlockSpec(shape, lambda b, i: (0,) * nd, pipeline_mode=pl.Buffered(1))


def _tile_spec(tile_rows, d):
    return pl.BlockSpec((None, tile_rows, d), lambda b, i: (b, i, 0))


def _compiler_params():
    return pltpu.CompilerParams(
        dimension_semantics=("arbitrary", "arbitrary"),
        vmem_limit_bytes=VMEM_LIMIT_BYTES,
    )


def _cast_blocks(n_rows, n_steps):
    for n_blocks in range(n_steps, 0, -1):
        if (n_steps % n_blocks == 0 and n_rows % n_blocks == 0
                and (n_rows // n_blocks) % BF16_SUBLANES == 0):
            return n_blocks
    raise ValueError(f"no row blocking of {n_rows} rows over {n_steps} steps")


def _sublayer_call(body, name, h, consts, scratch_shapes, *, tile_rows, cast_next=()):
    bsz, seq, d = h.shape
    n_tiles = seq // tile_rows
    n_steps = bsz * n_tiles
    n_in = 1 + len(consts)
    n_cast = len(cast_next)

    def cast_specs(w, layer):
        _, n_rows, n_cols = w.shape
        n_blocks = _cast_blocks(n_rows, n_steps)
        steps_per_block = n_steps // n_blocks
        block = lambda b, i: (b * n_tiles + i) // steps_per_block
        return (pl.BlockSpec((None, n_rows // n_blocks, n_cols), lambda b, i: (layer, block(b, i), 0)),
                pl.BlockSpec((n_rows // n_blocks, n_cols), lambda b, i: (block(b, i), 0)))

    def kernel(*refs):
        ins, cast_ins = refs[:n_in], refs[n_in:n_in + n_cast]
        out, cast_outs = refs[n_in + n_cast], refs[n_in + n_cast + 1:n_in + 2 * n_cast + 1]
        scratch = refs[n_in + 2 * n_cast + 1:]
        for src, dst in zip(cast_ins, cast_outs):
            dst[...] = src[...].astype(dst.dtype)
        body(*ins, out, *scratch)

    cast_in, cast_out = zip(*[cast_specs(w, layer) for w, layer in cast_next]) if cast_next else ((), ())
    outs = pl.pallas_call(
        kernel,
        grid=(bsz, n_tiles),
        in_specs=[_tile_spec(tile_rows, d)] + [_const_spec(c.shape) for c in consts] + list(cast_in),
        out_specs=[_tile_spec(tile_rows, d)] + list(cast_out),
        out_shape=[jax.ShapeDtypeStruct(h.shape, h.dtype)]
        + [jax.ShapeDtypeStruct(w.shape[1:], jnp.bfloat16) for w, _ in cast_next],
        scratch_shapes=scratch_shapes,
        compiler_params=_compiler_params(),
        name=name,
    )(h, *consts, *[w for w, _ in cast_next])
    return outs[0], tuple(outs[1:])


def _s5_kernel(h_ref, gain_ref, er_ref, ei_ref, fr_ref, fi_ref, pr_ref, pi_ref,
               tri_ref, ws_ref, wko_ref, dskip_ref, wglu_ref, o_ref,
               carry_ref, slab_ref, hp_ref):
    @pl.when(pl.program_id(1) == 0)
    def _():
        carry_ref[...] = jnp.zeros_like(carry_ref)

    bf16 = jnp.bfloat16
    f32 = jnp.float32
    tile_rows, d = h_ref.shape
    n_lane_tiles = d // LANES
    pairs_per_lane_tile = LANES // PAIR_CH
    n_pairs = d // PAIR_CH
    pair_modes = (PAIR_CH // S5_GROUP) * S5_STATE
    n_chunks = tile_rows // LAG
    n_cblk = n_chunks // CHUNK_BLOCK
    cb_rows = CHUNK_BLOCK + HALO_ROWS

    x = h_ref[...]
    u = _rmsnorm(x, gain_ref[...])
    part_rows = tile_rows // S5_ROW_PARTS
    part_chunks = n_chunks // S5_ROW_PARTS
    for hf in range(S5_ROW_PARTS):
        for j in range(n_lane_tiles):
            slab_ref[hf, j] = u[hf * part_rows:(hf + 1) * part_rows, j * LANES:(j + 1) * LANES]
    lag_rows = [[_pack_rows(jnp.concatenate(
        [slab_ref[hf, j, pl.ds(dd, part_chunks, stride=LAG), :] for hf in range(S5_ROW_PARTS)], axis=0))
        for j in range(n_lane_tiles)] for dd in range(LAG)]
    u2, s_re, s_im = [], [], []
    for k in range(n_pairs):
        j, q = divmod(k, pairs_per_lane_tile)
        lanes = slice(q * PAIR_CH, (q + 1) * PAIR_CH)
        u2k = pltpu.bitcast(
            jnp.concatenate([lag_rows[dd][j][:, lanes] for dd in range(LAG)], axis=1), bf16)
        u2.append(u2k)
        sk = jnp.dot(u2k, ws_ref[k], preferred_element_type=f32)
        s_re.append(sk[:, :pair_modes])
        s_im.append(sk[:, pair_modes:])
    sr = jnp.concatenate(s_re, axis=1).astype(bf16)
    si = jnp.concatenate(s_im, axis=1).astype(bf16)

    er, ei, fr, fi = er_ref[...], ei_ref[...], fr_ref[...], fi_ref[...]
    pr, pi = pr_ref[...], pi_ref[...]
    xr = jnp.dot(tri_ref[...], er * sr - ei * si, preferred_element_type=f32)
    xi = jnp.dot(tri_ref[...], er * si + ei * sr, preferred_element_type=f32)
    c_r, c_i = carry_ref[0:1, :], carry_ref[1:2, :]
    cs_r, cs_i = [], []
    for blk in range(n_cblk):
        cs_r.append(c_r)
        cs_i.append(c_i)
        tot = blk * cb_rows + CHUNK_BLOCK
        t_r = xr[tot:tot + 1, :] + c_r
        t_i = xi[tot:tot + 1, :] + c_i
        c_r, c_i = pr * t_r - pi * t_i, pr * t_i + pi * t_r
    carry_ref[0:1, :] = c_r
    carry_ref[1:2, :] = c_i

    def with_carry(v, cs):
        return jnp.concatenate(
            [v[blk * cb_rows:blk * cb_rows + CHUNK_BLOCK, :] + cs[blk] for blk in range(n_cblk)],
            axis=0).astype(bf16)

    gr, gi = with_carry(xr, cs_r), with_carry(xi, cs_i)
    hpr = fr * gr - fi * gi
    hpi = fr * gi + fi * gr
    for k in range(n_pairs):
        md = slice(k * pair_modes, (k + 1) * pair_modes)
        hp_ref[:, 2 * k * pair_modes:(2 * k + 1) * pair_modes] = hpr[:, md]
        hp_ref[:, (2 * k + 1) * pair_modes:(2 * k + 2) * pair_modes] = hpi[:, md]

    y2 = []
    for k in range(n_pairs):
        hp = hp_ref[:, 2 * k * pair_modes:(2 * k + 2) * pair_modes]
        y2.append(_pack_rows(jnp.dot(jnp.concatenate([u2[k], hp], axis=1), wko_ref[k],
                                     preferred_element_type=f32)))
    for dd in range(LAG):
        lanes = slice(dd * PAIR_CH, (dd + 1) * PAIR_CH)
        for j in range(n_lane_tiles):
            piece = jnp.concatenate(
                [y2[j * pairs_per_lane_tile + q][:, lanes] for q in range(pairs_per_lane_tile)],
                axis=1)
            rows_f32 = pltpu.bitcast(piece, bf16).astype(f32)
            for hf in range(S5_ROW_PARTS):
                slab_ref[hf, j, pl.ds(dd, part_chunks, stride=LAG), :] = (
                    rows_f32[hf * part_chunks:(hf + 1) * part_chunks, :])

    for hf in range(S5_ROW_PARTS):
        rows = slice(hf * part_rows, (hf + 1) * part_rows)
        y = jnp.concatenate([slab_ref[hf, j] for j in range(n_lane_tiles)], axis=1)
        z = jax.nn.gelu(y + dskip_ref[...] * u[rows, :]).astype(bf16)
        for c in range(d // COL_BLOCK):
            cols = slice(c * COL_BLOCK, (c + 1) * COL_BLOCK)
            gcols = slice(d + c * COL_BLOCK, d + (c + 1) * COL_BLOCK)
            za = jnp.dot(z, wglu_ref[:, cols], preferred_element_type=f32)
            zg = jnp.dot(z, wglu_ref[:, gcols], preferred_element_type=f32)
            o_ref[rows, cols] = x[rows, cols] + za * jax.nn.sigmoid(zg)


def _s5_layer(h, gain, tabs, tri, ws, wko, dskip, wglu, *, tile_rows, cast_next):
    d = h.shape[-1]
    n_modes = tabs[0].shape[1]
    scratch = [
        pltpu.VMEM((HALO_ROWS, n_modes), jnp.float32),
        pltpu.VMEM((S5_ROW_PARTS, d // LANES, tile_rows // S5_ROW_PARTS, LANES), jnp.float32),
        pltpu.VMEM((tile_rows // LAG, 2 * n_modes), jnp.bfloat16),
    ]
    return _sublayer_call(_s5_kernel, "s5_mixer", h, (gain, *tabs, tri, ws, wko, dskip, wglu),
                          scratch, tile_rows=tile_rows, cast_next=cast_next)


def _causal_conv_rows(val, halo_ref, w_ref, cols=slice(None)):
    tile_rows = val.shape[0]
    ext = jnp.concatenate([halo_ref[:, cols], val], axis=0)
    out = val * w_ref[CONV_WIDTH - 1:CONV_WIDTH, cols]
    for k in range(CONV_WIDTH - 1):
        shift = CONV_WIDTH - 1 - k
        out = out + ext[HALO_ROWS - shift:HALO_ROWS - shift + tile_rows, :] * w_ref[k:k + 1, cols]
    halo_ref[:, cols] = val[tile_rows - HALO_ROWS:, :]
    return out


def _zero_halo_at_sequence_start(halo_ref):
    @pl.when(pl.program_id(1) == 0)
    def _():
        halo_ref[...] = jnp.zeros_like(halo_ref)


def _ffn_kernel(h_ref, gain_ref, wup_ref, cw_ref, cb_ref, wdown_ref, fgain_ref, o_ref,
                halo_ref, *, d_ff, final_norm):
    _zero_halo_at_sequence_start(halo_ref)
    for r in range(h_ref.shape[0] // ROW_BLOCK):
        rows = slice(r * ROW_BLOCK, (r + 1) * ROW_BLOCK)
        x = h_ref[rows, :]
        u = _rmsnorm(x, gain_ref[...]).astype(jnp.bfloat16)
        g = jnp.dot(u, wup_ref[:, :d_ff], preferred_element_type=jnp.float32)
        v = jnp.dot(u, wup_ref[:, d_ff:], preferred_element_type=jnp.float32)
        g = _causal_conv_rows(g, halo_ref, cw_ref) + cb_ref[...]
        act = (jax.nn.silu(g) * v).astype(jnp.bfloat16)
        out = x + jnp.dot(act, wdown_ref[...], preferred_element_type=jnp.float32)
        if final_norm:
            out = _rmsnorm(out, fgain_ref[...])
        o_ref[rows, :] = out


def _ffn_layer(h, gain, wup, cw, cb, wdown, fgain, *, tile_rows, final_norm, cast_next=()):
    d_ff = wdown.shape[0]
    body = functools.partial(_ffn_kernel, d_ff=d_ff, final_norm=final_norm)
    scratch = [pltpu.VMEM((HALO_ROWS, d_ff), jnp.float32)]
    return _sublayer_call(body, "conv_ffn_final" if final_norm else "conv_ffn", h,
                          (gain, wup, cw, cb, wdown, fgain), scratch,
                          tile_rows=tile_rows, cast_next=cast_next)


def _shortconv_kernel(h_ref, gain_ref, win_ref, cw_ref, wout_ref, o_ref, halo_ref):
    _zero_halo_at_sequence_start(halo_ref)
    d = h_ref.shape[-1]
    for r in range(h_ref.shape[0] // SHORTCONV_ROW_BLOCK):
        rows = slice(r * SHORTCONV_ROW_BLOCK, (r + 1) * SHORTCONV_ROW_BLOCK)
        x = h_ref[rows, :]
        u = _rmsnorm(x, gain_ref[...]).astype(jnp.bfloat16)
        gated = []
        for c in range(d // COL_BLOCK):
            cols = slice(c * COL_BLOCK, (c + 1) * COL_BLOCK)
            b_gate, c_gate, hh = [
                jnp.dot(u, win_ref[:, k * d + c * COL_BLOCK:k * d + (c + 1) * COL_BLOCK],
                        preferred_element_type=jnp.float32) for k in range(3)]
            v = _causal_conv_rows(c_gate * hh, halo_ref, cw_ref, cols)
            gated.append((b_gate * v).astype(jnp.bfloat16))
        o_ref[rows, :] = x + jnp.dot(jnp.concatenate(gated, axis=1), wout_ref[...],
                                     preferred_element_type=jnp.float32)


def _shortconv_layer(h, gain, win, cw, wout, *, tile_rows, cast_next):
    scratch = [pltpu.VMEM((HALO_ROWS, h.shape[-1]), jnp.float32)]
    return _sublayer_call(_shortconv_kernel, "shortconv_mixer", h, (gain, win, cw, wout), scratch,
                          tile_rows=tile_rows, cast_next=cast_next)


def _s5_discretise(a_re, a_im, log_dt, b_re, b_im):
    f32 = jnp.float32
    lam_r = a_re.astype(f32)
    lam_i = a_im.astype(f32)
    dt = jnp.exp(log_dt.astype(f32))[:, None]
    mag = jnp.exp(lam_r * dt)
    ab_r = mag * jnp.cos(lam_i * dt)
    ab_i = mag * jnp.sin(lam_i * dt)
    den = lam_r * lam_r + lam_i * lam_i
    nr = ab_r - 1.0
    g_r = ((nr * lam_r + ab_i * lam_i) / den)[..., None]
    g_i = ((ab_i * lam_r - nr * lam_i) / den)[..., None]
    br = b_re.astype(f32)
    bi = b_im.astype(f32)
    return lam_r * dt, lam_i * dt, g_r * br - g_i * bi, g_r * bi + g_i * br


def _a_power(rate_r, rate_i, k):
    k = jnp.asarray(k, jnp.float32)
    mag = jnp.exp(rate_r * k)
    return mag * jnp.cos(rate_i * k), mag * jnp.sin(rate_i * k)


def _cmul(ar, ai, br, bi):
    return ar * br - ai * bi, ar * bi + ai * br


def _s5_tables(rate_r, rate_i, n_chunk_blocks):
    rr = rate_r.reshape(1, -1)
    ri = rate_i.reshape(1, -1)
    mid = CHUNK_BLOCK // 2
    c = jnp.arange(CHUNK_BLOCK)[:, None]
    er, ei = _a_power(rr, ri, LAG * (mid - c))
    fr, fi = _a_power(rr, ri, LAG * (c - 1 - mid))
    pr, pi = _a_power(rr, ri, LAG * CHUNK_BLOCK)
    rep = lambda t: jnp.tile(t.astype(jnp.bfloat16), (n_chunk_blocks, 1))
    return rep(er), rep(ei), rep(fr), rep(fi), pr, pi


def _s5_weights(rate_r, rate_i, bb_r, bb_i, c_re, c_im):
    f32 = jnp.float32
    hp = lax.Precision.HIGHEST
    n_groups, n_state, n_ch = bb_r.shape
    gpp = PAIR_CH // n_ch
    n_pairs = n_groups // gpp
    wide = LAG * PAIR_CH
    pair_modes = gpp * n_state
    lane = jnp.arange(wide)
    lane_lag, lane_g2, lane_ch = lane // PAIR_CH, (lane // n_ch) % gpp, lane % n_ch
    mode_g2 = jnp.arange(pair_modes) // n_state
    ch = jnp.arange(n_ch)
    lags = jnp.arange(LAG)
    spread = (ch[:, None] == lane_ch[None, :]).astype(f32)
    lag_spread = (lags[:, None] == lane_lag[None, :]).astype(f32)
    own = (mode_g2[:, None] == lane_g2[None, :]).astype(f32)
    causal = (lane_lag[None, :] >= lane_lag[:, None]).astype(f32)
    pr = rate_r.reshape(n_pairs, pair_modes)
    pi = rate_i.reshape(n_pairs, pair_modes)

    def pair_rows(w):
        return w.reshape((n_pairs, pair_modes) + w.shape[2:])

    cw_r = jnp.einsum('gop,oc->gpc', c_re.astype(f32), spread, precision=hp)
    cw_i = jnp.einsum('gop,oc->gpc', c_im.astype(f32), spread, precision=hp)
    ad_r, ad_i = [jnp.einsum('kqd,dc->kqc', t, lag_spread, precision=hp)
                  for t in _a_power(pr[..., None], pi[..., None], lags)]
    rout_r, rout_i = _cmul(ad_r, ad_i, pair_rows(cw_r) * own, pair_rows(cw_i) * own)
    bt_r = jnp.einsum('hc,kqh->kcq', spread, pair_rows(bb_r), precision=hp)
    bt_i = jnp.einsum('hc,kqh->kcq', spread, pair_rows(bb_i), precision=hp)
    an_r, an_i = [jnp.repeat(t, PAIR_CH, axis=1)
                  for t in _a_power(pr[:, None, :], pi[:, None, :], -lags[None, :, None])]
    lin_r, lin_i = _cmul(an_r, an_i, bt_r * own.T, bt_i * own.T)

    wk = (jnp.einsum('kcq,kqe->kce', lin_r, rout_r, precision=hp)
          - jnp.einsum('kcq,kqe->kce', lin_i, rout_i, precision=hp)) * causal
    ws_r, ws_i = _cmul(*_a_power(pr[:, None, :], pi[:, None, :], LAG - 1), lin_r, lin_i)
    wo_r, wo_i = _cmul(*_a_power(pr[..., None], pi[..., None], 1), rout_r, rout_i)

    ws = jnp.concatenate([ws_r, ws_i], axis=2)
    wko = jnp.concatenate([wk, wo_r, -wo_i], axis=1)
    return ws.astype(jnp.bfloat16), wko.astype(jnp.bfloat16)


def _chunk_prefix_matrix(n_chunk_blocks):
    r = jnp.arange(CHUNK_BLOCK + HALO_ROWS)[:, None]
    c = jnp.arange(CHUNK_BLOCK)[None, :]
    one_block = ((c < r) & (r <= CHUNK_BLOCK)).astype(jnp.float32)
    return jnp.kron(jnp.eye(n_chunk_blocks, dtype=jnp.float32), one_block).astype(jnp.bfloat16)


def kernel(x, norm_mix, norm_ffn, norm_final, s5_a_re, s5_a_im, s5_log_dt, s5_b_re, s5_b_im,
           s5_c_re, s5_c_im, s5_d, s5_w_glu, sc_w_in, sc_conv_w, sc_w_out,
           ffn_w_up, ffn_conv_w, ffn_conv_b, ffn_w_down):
    bf16 = jnp.bfloat16
    row = lambda v: v.reshape(1, -1)

    rate_r, rate_i, bb_r, bb_i = _s5_discretise(s5_a_re[0], s5_a_im[0], s5_log_dt[0],
                                                 s5_b_re[0], s5_b_im[0])
    n_chunk_blocks = TILE_ROWS // (LAG * CHUNK_BLOCK)
    tabs = _s5_tables(rate_r, rate_i, n_chunk_blocks)
    ws, wko = _s5_weights(rate_r, rate_i, bb_r, bb_i, s5_c_re[0], s5_c_im[0])
    h, (wup0, wdown0) = _s5_layer(
        x, row(norm_mix[0]), tabs, _chunk_prefix_matrix(n_chunk_blocks), ws, wko, row(s5_d[0]),
        s5_w_glu[0].astype(bf16), tile_rows=TILE_ROWS,
        cast_next=((ffn_w_up, 0), (ffn_w_down, 0)))
    h, (win, wout) = _ffn_layer(
        h, row(norm_ffn[0]), wup0, ffn_conv_w[0], row(ffn_conv_b[0]), wdown0, row(norm_final),
        tile_rows=TILE_ROWS, final_norm=False, cast_next=((sc_w_in, 0), (sc_w_out, 0)))
    h, (wup1, wdown1) = _shortconv_layer(
        h, row(norm_mix[1]), win, sc_conv_w[0], wout, tile_rows=2 * TILE_ROWS,
        cast_next=((ffn_w_up, 1), (ffn_w_down, 1)))
    h, _ = _ffn_layer(
        h, row(norm_ffn[1]), wup1, ffn_conv_w[1], row(ffn_conv_b[1]), wdown1, row(norm_final),
        tile_rows=TILE_ROWS, final_norm=True)
    return h
```

```python
import functools

import jax
import jax.numpy as jnp
from jax import lax
from jax.experimental import pallas as pl
from jax.experimental.pallas import tpu as pltpu

RMS_EPS = 1e-6
S5_GROUP = 16
S5_STATE = 64
CONV_WIDTH = 3
HALO_ROWS = 8

LANES = 128
BF16_SUBLANES = 16
LAG = 8
PAIR_CH = 32
CHUNK_BLOCK = 16
COL_BLOCK = 256
ROW_BLOCK = 256
SHORTCONV_ROW_BLOCK = 1024
TILE_ROWS = 1024
S5_ROW_PARTS = 2
VMEM_LIMIT_BYTES = 56 * 1024 * 1024


def _rmsnorm(x, gain):
    ms = jnp.mean(x * x, axis=-1, keepdims=True)
    return x * lax.rsqrt(ms + RMS_EPS) * gain


def _pack_rows(v):
    return pltpu.bitcast(v.astype(jnp.bfloat16), jnp.uint32)


def _const_spec(shape):
    nd = len(shape)
    return pl.BlockSpec(shape, lambda b, i: (0,) * nd, pipeline_mode=pl.Buffered(1))


def _tile_spec(tile_rows, d):
    return pl.BlockSpec((None, tile_rows, d), lambda b, i: (b, i, 0))


def _compiler_params():
    return pltpu.CompilerParams(
        dimension_semantics=("arbitrary", "arbitrary"),
        vmem_limit_bytes=VMEM_LIMIT_BYTES,
    )


def _cast_blocks(n_rows, n_steps):
    for n_blocks in range(n_steps, 0, -1):
        if (n_steps % n_blocks == 0 and n_rows % n_blocks == 0
                and (n_rows // n_blocks) % BF16_SUBLANES == 0):
            return n_blocks
    raise ValueError(f"no row blocking of {n_rows} rows over {n_steps} steps")


def _sublayer_call(body, name, h, consts, scratch_shapes, *, tile_rows, cast_next=()):
    bsz, seq, d = h.shape
    n_tiles = seq // tile_rows
    n_steps = bsz * n_tiles
    n_in = 1 + len(consts)
    n_cast = len(cast_next)

    def cast_specs(w, layer):
        _, n_rows, n_cols = w.shape
        n_blocks = _cast_blocks(n_rows, n_steps)
        steps_per_block = n_steps // n_blocks
        block = lambda b, i: (b * n_tiles + i) // steps_per_block
        return (pl.BlockSpec((None, n_rows // n_blocks, n_cols), lambda b, i: (layer, block(b, i), 0)),
                pl.BlockSpec((n_rows // n_blocks, n_cols), lambda b, i: (block(b, i), 0)))

    def kernel(*refs):
        ins, cast_ins = refs[:n_in], refs[n_in:n_in + n_cast]
        out, cast_outs = refs[n_in + n_cast], refs[n_in + n_cast + 1:n_in + 2 * n_cast + 1]
        scratch = refs[n_in + 2 * n_cast + 1:]
        for src, dst in zip(cast_ins, cast_outs):
            dst[...] = src[...].astype(dst.dtype)
        body(*ins, out, *scratch)

    cast_in, cast_out = zip(*[cast_specs(w, layer) for w, layer in cast_next]) if cast_next else ((), ())
    outs = pl.pallas_call(
        kernel,
        grid=(bsz, n_tiles),
        in_specs=[_tile_spec(tile_rows, d)] + [_const_spec(c.shape) for c in consts] + list(cast_in),
        out_specs=[_tile_spec(tile_rows, d)] + list(cast_out),
        out_shape=[jax.ShapeDtypeStruct(h.shape, h.dtype)]
        + [jax.ShapeDtypeStruct(w.shape[1:], jnp.bfloat16) for w, _ in cast_next],
        scratch_shapes=scratch_shapes,
        compiler_params=_compiler_params(),
        name=name,
    )(h, *consts, *[w for w, _ in cast_next])
    return outs[0], tuple(outs[1:])


def _s5_kernel(h_ref, gain_ref, er_ref, ei_ref, fr_ref, fi_ref, pr_ref, pi_ref,
               tri_ref, ws_ref, wko_ref, dskip_ref, wglu_ref, o_ref,
               carry_ref, slab_ref, hp_ref):
    @pl.when(pl.program_id(1) == 0)
    def _():
        carry_ref[...] = jnp.zeros_like(carry_ref)

    bf16 = jnp.bfloat16
    f32 = jnp.float32
    tile_rows, d = h_ref.shape
    n_lane_tiles = d // LANES
    pairs_per_lane_tile = LANES // PAIR_CH
    n_pairs = d // PAIR_CH
    pair_modes = (PAIR_CH // S5_GROUP) * S5_STATE
    n_chunks = tile_rows // LAG
    n_cblk = n_chunks // CHUNK_BLOCK
    cb_rows = CHUNK_BLOCK + HALO_ROWS

    x = h_ref[...]
    u = _rmsnorm(x, gain_ref[...])
    part_rows = tile_rows // S5_ROW_PARTS
    part_chunks = n_chunks // S5_ROW_PARTS
    for hf in range(S5_ROW_PARTS):
        for j in range(n_lane_tiles):
            slab_ref[hf, j] = u[hf * part_rows:(hf + 1) * part_rows, j * LANES:(j + 1) * LANES]
    lag_rows = [[_pack_rows(jnp.concatenate(
        [slab_ref[hf, j, pl.ds(dd, part_chunks, stride=LAG), :] for hf in range(S5_ROW_PARTS)], axis=0))
        for j in range(n_lane_tiles)] for dd in range(LAG)]
    u2, s_re, s_im = [], [], []
    for k in range(n_pairs):
        j, q = divmod(k, pairs_per_lane_tile)
        lanes = slice(q * PAIR_CH, (q + 1) * PAIR_CH)
        u2k = pltpu.bitcast(
            jnp.concatenate([lag_rows[dd][j][:, lanes] for dd in range(LAG)], axis=1), bf16)
        u2.append(u2k)
        sk = jnp.dot(u2k, ws_ref[k], preferred_element_type=f32)
        s_re.append(sk[:, :pair_modes])
        s_im.append(sk[:, pair_modes:])
    sr = jnp.concatenate(s_re, axis=1).astype(bf16)
    si = jnp.concatenate(s_im, axis=1).astype(bf16)

    er, ei, fr, fi = er_ref[...], ei_ref[...], fr_ref[...], fi_ref[...]
    pr, pi = pr_ref[...], pi_ref[...]
    xr = jnp.dot(tri_ref[...], er * sr - ei * si, preferred_element_type=f32)
    xi = jnp.dot(tri_ref[...], er * si + ei * sr, preferred_element_type=f32)
    c_r, c_i = carry_ref[0:1, :], carry_ref[1:2, :]
    cs_r, cs_i = [], []
    for blk in range(n_cblk):
        cs_r.append(c_r)
        cs_i.append(c_i)
        tot = blk * cb_rows + CHUNK_BLOCK
        t_r = xr[tot:tot + 1, :] + c_r
        t_i = xi[tot:tot + 1, :] + c_i
        c_r, c_i = pr * t_r - pi * t_i, pr * t_i + pi * t_r
    carry_ref[0:1, :] = c_r
    carry_ref[1:2, :] = c_i

    def with_carry(v, cs):
        return jnp.concatenate(
            [v[blk * cb_rows:blk * cb_rows + CHUNK_BLOCK, :] + cs[blk] for blk in range(n_cblk)],
            axis=0).astype(bf16)

    gr, gi = with_carry(xr, cs_r), with_carry(xi, cs_i)
    hpr = fr * gr - fi * gi
    hpi = fr * gi + fi * gr
    for k in range(n_pairs):
        md = slice(k * pair_modes, (k + 1) * pair_modes)
        hp_ref[:, 2 * k * pair_modes:(2 * k + 1) * pair_modes] = hpr[:, md]
        hp_ref[:, (2 * k + 1) * pair_modes:(2 * k + 2) * pair_modes] = hpi[:, md]

    y2 = []
    for k in range(n_pairs):
        hp = hp_ref[:, 2 * k * pair_modes:(2 * k + 2) * pair_modes]
        y2.append(_pack_rows(jnp.dot(jnp.concatenate([u2[k], hp], axis=1), wko_ref[k],
                                     preferred_element_type=f32)))
    for dd in range(LAG):
        lanes = slice(dd * PAIR_CH, (dd + 1) * PAIR_CH)
        for j in range(n_lane_tiles):
            piece = jnp.concatenate(
                [y2[j * pairs_per_lane_tile + q][:, lanes] for q in range(pairs_per_lane_tile)],
                axis=1)
            rows_f32 = pltpu.bitcast(piece, bf16).astype(f32)
            for hf in range(S5_ROW_PARTS):
                slab_ref[hf, j, pl.ds(dd, part_chunks, stride=LAG), :] = (
                    rows_f32[hf * part_chunks:(hf + 1) * part_chunks, :])

    for hf in range(S5_ROW_PARTS):
        rows = slice(hf * part_rows, (hf + 1) * part_rows)
        y = jnp.concatenate([slab_ref[hf, j] for j in range(n_lane_tiles)], axis=1)
        z = jax.nn.gelu(y + dskip_ref[...] * u[rows, :]).astype(bf16)
        for c in range(d // COL_BLOCK):
            cols = slice(c * COL_BLOCK, (c + 1) * COL_BLOCK)
            gcols = slice(d + c * COL_BLOCK, d + (c + 1) * COL_BLOCK)
            za = jnp.dot(z, wglu_ref[:, cols], preferred_element_type=f32)
            zg = jnp.dot(z, wglu_ref[:, gcols], preferred_element_type=f32)
            o_ref[rows, cols] = x[rows, cols] + za * jax.nn.sigmoid(zg)


def _s5_layer(h, gain, tabs, tri, ws, wko, dskip, wglu, *, tile_rows, cast_next):
    d = h.shape[-1]
    n_modes = tabs[0].shape[1]
    scratch = [
        pltpu.VMEM((HALO_ROWS, n_modes), jnp.float32),
        pltpu.VMEM((S5_ROW_PARTS, d // LANES, tile_rows // S5_ROW_PARTS, LANES), jnp.float32),
        pltpu.VMEM((tile_rows // LAG, 2 * n_modes), jnp.bfloat16),
    ]
    return _sublayer_call(_s5_kernel, "s5_mixer", h, (gain, *tabs, tri, ws, wko, dskip, wglu),
                          scratch, tile_rows=tile_rows, cast_next=cast_next)


def _causal_conv_rows(val, halo_ref, w_ref, cols=slice(None)):
    tile_rows = val.shape[0]
    ext = jnp.concatenate([halo_ref[:, cols], val], axis=0)
    out = val * w_ref[CONV_WIDTH - 1:CONV_WIDTH, cols]
    for k in range(CONV_WIDTH - 1):
        shift = CONV_WIDTH - 1 - k
        out = out + ext[HALO_ROWS - shift:HALO_ROWS - shift + tile_rows, :] * w_ref[k:k + 1, cols]
    halo_ref[:, cols] = val[tile_rows - HALO_ROWS:, :]
    return out


def _zero_halo_at_sequence_start(halo_ref):
    @pl.when(pl.program_id(1) == 0)
    def _():
        halo_ref[...] = jnp.zeros_like(halo_ref)


def _ffn_kernel(h_ref, gain_ref, wup_ref, cw_ref, cb_ref, wdown_ref, fgain_ref, o_ref,
                halo_ref, *, d_ff, final_norm):
    _zero_halo_at_sequence_start(halo_ref)
    for r in range(h_ref.shape[0] // ROW_BLOCK):
        rows = slice(r * ROW_BLOCK, (r + 1) * ROW_BLOCK)
        x = h_ref[rows, :]
        u = _rmsnorm(x, gain_ref[...]).astype(jnp.bfloat16)
        g = jnp.dot(u, wup_ref[:, :d_ff], preferred_element_type=jnp.float32)
        v = jnp.dot(u, wup_ref[:, d_ff:], preferred_element_type=jnp.float32)
        g = _causal_conv_rows(g, halo_ref, cw_ref) + cb_ref[...]
        act = (jax.nn.silu(g) * v).astype(jnp.bfloat16)
        out = x + jnp.dot(act, wdown_ref[...], preferred_element_type=jnp.float32)
        if final_norm:
            out = _rmsnorm(out, fgain_ref[...])
        o_ref[rows, :] = out


def _ffn_layer(h, gain, wup, cw, cb, wdown, fgain, *, tile_rows, final_norm, cast_next=()):
    d_ff = wdown.shape[0]
    body = functools.partial(_ffn_kernel, d_ff=d_ff, final_norm=final_norm)
    scratch = [pltpu.VMEM((HALO_ROWS, d_ff), jnp.float32)]
    return _sublayer_call(body, "conv_ffn_final" if final_norm else "conv_ffn", h,
                          (gain, wup, cw, cb, wdown, fgain), scratch,
                          tile_rows=tile_rows, cast_next=cast_next)


def _shortconv_kernel(h_ref, gain_ref, win_ref, cw_ref, wout_ref, o_ref, halo_ref):
    _zero_halo_at_sequence_start(halo_ref)
    d = h_ref.shape[-1]
    for r in range(h_ref.shape[0] // SHORTCONV_ROW_BLOCK):
        rows = slice(r * SHORTCONV_ROW_BLOCK, (r + 1) * SHORTCONV_ROW_BLOCK)
        x = h_ref[rows, :]
        u = _rmsnorm(x, gain_ref[...]).astype(jnp.bfloat16)
        gated = []
        for c in range(d // COL_BLOCK):
            cols = slice(c * COL_BLOCK, (c + 1) * COL_BLOCK)
            b_gate, c_gate, hh = [
                jnp.dot(u, win_ref[:, k * d + c * COL_BLOCK:k * d + (c + 1) * COL_BLOCK],
                        preferred_element_type=jnp.float32) for k in range(3)]
            v = _causal_conv_rows(c_gate * hh, halo_ref, cw_ref, cols)
            gated.append((b_gate * v).astype(jnp.bfloat16))
        o_ref[rows, :] = x + jnp.dot(jnp.concatenate(gated, axis=1), wout_ref[...],
                                     preferred_element_type=jnp.float32)


def _shortconv_layer(h, gain, win, cw, wout, *, tile_rows, cast_next):
    scratch = [pltpu.VMEM((HALO_ROWS, h.shape[-1]), jnp.float32)]
    return _sublayer_call(_shortconv_kernel, "shortconv_mixer", h, (gain, win, cw, wout), scratch,
                          tile_rows=tile_rows, cast_next=cast_next)


def _s5_discretise(a_re, a_im, log_dt, b_re, b_im):
    f32 = jnp.float32
    lam_r = a_re.astype(f32)
    lam_i = a_im.astype(f32)
    dt = jnp.exp(log_dt.astype(f32))[:, None]
    mag = jnp.exp(lam_r * dt)
    ab_r = mag * jnp.cos(lam_i * dt)
    ab_i = mag * jnp.sin(lam_i * dt)
    den = lam_r * lam_r + lam_i * lam_i
    nr = ab_r - 1.0
    g_r = ((nr * lam_r + ab_i * lam_i) / den)[..., None]
    g_i = ((ab_i * lam_r - nr * lam_i) / den)[..., None]
    br = b_re.astype(f32)
    bi = b_im.astype(f32)
    return lam_r * dt, lam_i * dt, g_r * br - g_i * bi, g_r * bi + g_i * br


def _a_power(rate_r, rate_i, k):
    k = jnp.asarray(k, jnp.float32)
    mag = jnp.exp(rate_r * k)
    return mag * jnp.cos(rate_i * k), mag * jnp.sin(rate_i * k)


def _cmul(ar, ai, br, bi):
    return ar * br - ai * bi, ar * bi + ai * br


def _s5_tables(rate_r, rate_i, n_chunk_blocks):
    rr = rate_r.reshape(1, -1)
    ri = rate_i.reshape(1, -1)
    mid = CHUNK_BLOCK // 2
    c = jnp.arange(CHUNK_BLOCK)[:, None]
    er, ei = _a_power(rr, ri, LAG * (mid - c))
    fr, fi = _a_power(rr, ri, LAG * (c - 1 - mid))
    pr, pi = _a_power(rr, ri, LAG * CHUNK_BLOCK)
    rep = lambda t: jnp.tile(t.astype(jnp.bfloat16), (n_chunk_blocks, 1))
    return rep(er), rep(ei), rep(fr), rep(fi), pr, pi


def _s5_weights(rate_r, rate_i, bb_r, bb_i, c_re, c_im):
    f32 = jnp.float32
    hp = lax.Precision.HIGHEST
    n_groups, n_state, n_ch = bb_r.shape
    gpp = PAIR_CH // n_ch
    n_pairs = n_groups // gpp
    wide = LAG * PAIR_CH
    pair_modes = gpp * n_state
    lane = jnp.arange(wide)
    lane_lag, lane_g2, lane_ch = lane // PAIR_CH, (lane // n_ch) % gpp, lane % n_ch
    mode_g2 = jnp.arange(pair_modes) // n_state
    ch = jnp.arange(n_ch)
    lags = jnp.arange(LAG)
    spread = (ch[:, None] == lane_ch[None, :]).astype(f32)
    lag_spread = (lags[:, None] == lane_lag[None, :]).astype(f32)
    own = (mode_g2[:, None] == lane_g2[None, :]).astype(f32)
    causal = (lane_lag[None, :] >= lane_lag[:, None]).astype(f32)
    pr = rate_r.reshape(n_pairs, pair_modes)
    pi = rate_i.reshape(n_pairs, pair_modes)

    def pair_rows(w):
        return w.reshape((n_pairs, pair_modes) + w.shape[2:])

    cw_r = jnp.einsum('gop,oc->gpc', c_re.astype(f32), spread, precision=hp)
    cw_i = jnp.einsum('gop,oc->gpc', c_im.astype(f32), spread, precision=hp)
    ad_r, ad_i = [jnp.einsum('kqd,dc->kqc', t, lag_spread, precision=hp)
                  for t in _a_power(pr[..., None], pi[..., None], lags)]
    rout_r, rout_i = _cmul(ad_r, ad_i, pair_rows(cw_r) * own, pair_rows(cw_i) * own)
    bt_r = jnp.einsum('hc,kqh->kcq', spread, pair_rows(bb_r), precision=hp)
    bt_i = jnp.einsum('hc,kqh->kcq', spread, pair_rows(bb_i), precision=hp)
    an_r, an_i = [jnp.repeat(t, PAIR_CH, axis=1)
                  for t in _a_power(pr[:, None, :], pi[:, None, :], -lags[None, :, None])]
    lin_r, lin_i = _cmul(an_r, an_i, bt_r * own.T, bt_i * own.T)

    wk = (jnp.einsum('kcq,kqe->kce', lin_r, rout_r, precision=hp)
          - jnp.einsum('kcq,kqe->kce', lin_i, rout_i, precision=hp)) * causal
    ws_r, ws_i = _cmul(*_a_power(pr[:, None, :], pi[:, None, :], LAG - 1), lin_r, lin_i)
    wo_r, wo_i = _cmul(*_a_power(pr[..., None], pi[..., None], 1), rout_r, rout_i)

    ws = jnp.concatenate([ws_r, ws_i], axis=2)
    wko = jnp.concatenate([wk, wo_r, -wo_i], axis=1)
    return ws.astype(jnp.bfloat16), wko.astype(jnp.bfloat16)


def _chunk_prefix_matrix(n_chunk_blocks):
    r = jnp.arange(CHUNK_BLOCK + HALO_ROWS)[:, None]
    c = jnp.arange(CHUNK_BLOCK)[None, :]
    one_block = ((c < r) & (r <= CHUNK_BLOCK)).astype(jnp.float32)
    return jnp.kron(jnp.eye(n_chunk_blocks, dtype=jnp.float32), one_block).astype(jnp.bfloat16)


def kernel(x, norm_mix, norm_ffn, norm_final, s5_a_re, s5_a_im, s5_log_dt, s5_b_re, s5_b_im,
           s5_c_re, s5_c_im, s5_d, s5_w_glu, sc_w_in, sc_conv_w, sc_w_out,
           ffn_w_up, ffn_conv_w, ffn_conv_b, ffn_w_down):
    bf16 = jnp.bfloat16
    row = lambda v: v.reshape(1, -1)

    rate_r, rate_i, bb_r, bb_i = _s5_discretise(s5_a_re[0], s5_a_im[0], s5_log_dt[0],
                                                 s5_b_re[0], s5_b_im[0])
    n_chunk_blocks = TILE_ROWS // (LAG * CHUNK_BLOCK)
    tabs = _s5_tables(rate_r, rate_i, n_chunk_blocks)
    ws, wko = _s5_weights(rate_r, rate_i, bb_r, bb_i, s5_c_re[0], s5_c_im[0])
    h, (wup0, wdown0) = _s5_layer(
        x, row(norm_mix[0]), tabs, _chunk_prefix_matrix(n_chunk_blocks), ws, wko, row(s5_d[0]),
        s5_w_glu[0].astype(bf16), tile_rows=TILE_ROWS,
        cast_next=((ffn_w_up, 0), (ffn_w_down, 0)))
    h, (win, wout) = _ffn_layer(
        h, row(norm_ffn[0]), wup0, ffn_conv_w[0], row(ffn_conv_b[0]), wdown0, row(norm_final),
        tile_rows=TILE_ROWS, final_norm=False, cast_next=((sc_w_in, 0), (sc_w_out, 0)))
    h, (wup1, wdown1) = _shortconv_layer(
        h, row(norm_mix[1]), win, sc_conv_w[0], wout, tile_rows=2 * TILE_ROWS,
        cast_next=((ffn_w_up, 1), (ffn_w_down, 1)))
    h, _ = _ffn_layer(
        h, row(norm_ffn[1]), wup1, ffn_conv_w[1], row(ffn_conv_b[1]), wdown1, row(norm_final),
        tile_rows=TILE_ROWS, final_norm=True)
    return h
```

```python
import functools

import jax
import jax.numpy as jnp
from jax import lax
from jax.experimental import pallas as pl
from jax.experimental.pallas import tpu as pltpu

RMS_EPS = 1e-6
S5_GROUP = 16
S5_STATE = 64
CONV_WIDTH = 3
HALO_ROWS = 8

LANES = 128
BF16_SUBLANES = 16
LAG = 8
PAIR_CH = 32
CHUNK_BLOCK = 16
COL_BLOCK = 256
ROW_BLOCK = 256
SHORTCONV_ROW_BLOCK = 512
TILE_ROWS = 1024
S5_ROW_PARTS = 2
VMEM_LIMIT_BYTES = 56 * 1024 * 1024


def _rmsnorm(x, gain):
    ms = jnp.mean(x * x, axis=-1, keepdims=True)
    return x * lax.rsqrt(ms + RMS_EPS) * gain


def _pack_rows(v):
    return pltpu.bitcast(v.astype(jnp.bfloat16), jnp.uint32)


def _const_spec(shape):
    nd = len(shape)
    return pl.BlockSpec(shape, lambda b, i: (0,) * nd, pipeline_mode=pl.Buffered(1))


def _tile_spec(tile_rows, d):
    return pl.BlockSpec((None, tile_rows, d), lambda b, i: (b, i, 0))


def _compiler_params():
    return pltpu.CompilerParams(
        dimension_semantics=("arbitrary", "arbitrary"),
        vmem_limit_bytes=VMEM_LIMIT_BYTES,
    )


def _cast_blocks(n_rows, n_steps):
    for n_blocks in range(n_steps, 0, -1):
        if (n_steps % n_blocks == 0 and n_rows % n_blocks == 0
                and (n_rows // n_blocks) % BF16_SUBLANES == 0):
            return n_blocks
    raise ValueError(f"no row blocking of {n_rows} rows over {n_steps} steps")


def _sublayer_call(body, name, h, consts, scratch_shapes, *, tile_rows, cast_next=()):
    bsz, seq, d = h.shape
    n_tiles = seq // tile_rows
    n_steps = bsz * n_tiles
    n_in = 1 + len(consts)
    n_cast = len(cast_next)

    def cast_specs(w, layer):
        _, n_rows, n_cols = w.shape
        n_blocks = _cast_blocks(n_rows, n_steps)
        steps_per_block = n_steps // n_blocks
        block = lambda b, i: (b * n_tiles + i) // steps_per_block
        return (pl.BlockSpec((None, n_rows // n_blocks, n_cols), lambda b, i: (layer, block(b, i), 0)),
                pl.BlockSpec((n_rows // n_blocks, n_cols), lambda b, i: (block(b, i), 0)))

    def kernel(*refs):
        ins, cast_ins = refs[:n_in], refs[n_in:n_in + n_cast]
        out, cast_outs = refs[n_in + n_cast], refs[n_in + n_cast + 1:n_in + 2 * n_cast + 1]
        scratch = refs[n_in + 2 * n_cast + 1:]
        for src, dst in zip(cast_ins, cast_outs):
            dst[...] = src[...].astype(dst.dtype)
        body(*ins, out, *scratch)

    cast_in, cast_out = zip(*[cast_specs(w, layer) for w, layer in cast_next]) if cast_next else ((), ())
    outs = pl.pallas_call(
        kernel,
        grid=(bsz, n_tiles),
        in_specs=[_tile_spec(tile_rows, d)] + [_const_spec(c.shape) for c in consts] + list(cast_in),
        out_specs=[_tile_spec(tile_rows, d)] + list(cast_out),
        out_shape=[jax.ShapeDtypeStruct(h.shape, h.dtype)]
        + [jax.ShapeDtypeStruct(w.shape[1:], jnp.bfloat16) for w, _ in cast_next],
        scratch_shapes=scratch_shapes,
        compiler_params=_compiler_params(),
        name=name,
    )(h, *consts, *[w for w, _ in cast_next])
    return outs[0], tuple(outs[1:])


def _s5_kernel(h_ref, gain_ref, er_ref, ei_ref, fr_ref, fi_ref, pr_ref, pi_ref,
               tri_ref, ws_ref, wko_ref, dskip_ref, wglu_ref, o_ref,
               carry_ref, slab_ref, hp_ref):
    @pl.when(pl.program_id(1) == 0)
    def _():
        carry_ref[...] = jnp.zeros_like(carry_ref)

    bf16 = jnp.bfloat16
    f32 = jnp.float32
    tile_rows, d = h_ref.shape
    n_lane_tiles = d // LANES
    pairs_per_lane_tile = LANES // PAIR_CH
    n_pairs = d // PAIR_CH
    pair_modes = (PAIR_CH // S5_GROUP) * S5_STATE
    n_chunks = tile_rows // LAG
    n_cblk = n_chunks // CHUNK_BLOCK
    cb_rows = CHUNK_BLOCK + HALO_ROWS

    x = h_ref[...]
    u = _rmsnorm(x, gain_ref[...])
    part_rows = tile_rows // S5_ROW_PARTS
    part_chunks = n_chunks // S5_ROW_PARTS
    for hf in range(S5_ROW_PARTS):
        for j in range(n_lane_tiles):
            slab_ref[hf, j] = u[hf * part_rows:(hf + 1) * part_rows, j * LANES:(j + 1) * LANES]
    lag_rows = [[_pack_rows(jnp.concatenate(
        [slab_ref[hf, j, pl.ds(dd, part_chunks, stride=LAG), :] for hf in range(S5_ROW_PARTS)], axis=0))
        for j in range(n_lane_tiles)] for dd in range(LAG)]
    u2, s_re, s_im = [], [], []
    for k in range(n_pairs):
        j, q = divmod(k, pairs_per_lane_tile)
        lanes = slice(q * PAIR_CH, (q + 1) * PAIR_CH)
        u2k = pltpu.bitcast(
            jnp.concatenate([lag_rows[dd][j][:, lanes] for dd in range(LAG)], axis=1), bf16)
        u2.append(u2k)
        sk = jnp.dot(u2k, ws_ref[k], preferred_element_type=f32)
        s_re.append(sk[:, :pair_modes])
        s_im.append(sk[:, pair_modes:])
    sr = jnp.concatenate(s_re, axis=1).astype(bf16)
    si = jnp.concatenate(s_im, axis=1).astype(bf16)

    er, ei, fr, fi = er_ref[...], ei_ref[...], fr_ref[...], fi_ref[...]
    pr, pi = pr_ref[...], pi_ref[...]
    xr = jnp.dot(tri_ref[...], er * sr - ei * si, preferred_element_type=f32)
    xi = jnp.dot(tri_ref[...], er * si + ei * sr, preferred_element_type=f32)
    c_r, c_i = carry_ref[0:1, :], carry_ref[1:2, :]
    cs_r, cs_i = [], []
    for blk in range(n_cblk):
        cs_r.append(c_r)
        cs_i.append(c_i)
        tot = blk * cb_rows + CHUNK_BLOCK
        t_r = xr[tot:tot + 1, :] + c_r
        t_i = xi[tot:tot + 1, :] + c_i
        c_r, c_i = pr * t_r - pi * t_i, pr * t_i + pi * t_r
    carry_ref[0:1, :] = c_r
    carry_ref[1:2, :] = c_i

    def with_carry(v, cs):
        return jnp.concatenate(
            [v[blk * cb_rows:blk * cb_rows + CHUNK_BLOCK, :] + cs[blk] for blk in range(n_cblk)],
            axis=0).astype(bf16)

    gr, gi = with_carry(xr, cs_r), with_carry(xi, cs_i)
    hpr = fr * gr - fi * gi
    hpi = fr * gi + fi * gr
    for k in range(n_pairs):
        md = slice(k * pair_modes, (k + 1) * pair_modes)
        hp_ref[:, 2 * k * pair_modes:(2 * k + 1) * pair_modes] = hpr[:, md]
        hp_ref[:, (2 * k + 1) * pair_modes:(2 * k + 2) * pair_modes] = hpi[:, md]

    y2 = []
    for k in range(n_pairs):
        hp = hp_ref[:, 2 * k * pair_modes:(2 * k + 2) * pair_modes]
        y2.append(_pack_rows(jnp.dot(jnp.concatenate([u2[k], hp], axis=1), wko_ref[k],
                                     preferred_element_type=f32)))
    for dd in range(LAG):
        lanes = slice(dd * PAIR_CH, (dd + 1) * PAIR_CH)
        for j in range(n_lane_tiles):
            piece = jnp.concatenate(
                [y2[j * pairs_per_lane_tile + q][:, lanes] for q in range(pairs_per_lane_tile)],
                axis=1)
            rows_f32 = pltpu.bitcast(piece, bf16).astype(f32)
            for hf in range(S5_ROW_PARTS):
                slab_ref[hf, j, pl.ds(dd, part_chunks, stride=LAG), :] = (
                    rows_f32[hf * part_chunks:(hf + 1) * part_chunks, :])

    for hf in range(S5_ROW_PARTS):
        rows = slice(hf * part_rows, (hf + 1) * part_rows)
        y = jnp.concatenate([slab_ref[hf, j] for j in range(n_lane_tiles)], axis=1)
        z = jax.nn.gelu(y + dskip_ref[...] * u[rows, :]).astype(bf16)
        for c in range(d // COL_BLOCK):
            cols = slice(c * COL_BLOCK, (c + 1) * COL_BLOCK)
            gcols = slice(d + c * COL_BLOCK, d + (c + 1) * COL_BLOCK)
            za = jnp.dot(z, wglu_ref[:, cols], preferred_element_type=f32)
            zg = jnp.dot(z, wglu_ref[:, gcols], preferred_element_type=f32)
            o_ref[rows, cols] = x[rows, cols] + za * jax.nn.sigmoid(zg)


def _s5_layer(h, gain, tabs, tri, ws, wko, dskip, wglu, *, tile_rows, cast_next):
    d = h.shape[-1]
    n_modes = tabs[0].shape[1]
    scratch = [
        pltpu.VMEM((HALO_ROWS, n_modes), jnp.float32),
        pltpu.VMEM((S5_ROW_PARTS, d // LANES, tile_rows // S5_ROW_PARTS, LANES), jnp.float32),
        pltpu.VMEM((tile_rows // LAG, 2 * n_modes), jnp.bfloat16),
    ]
    return _sublayer_call(_s5_kernel, "s5_mixer", h, (gain, *tabs, tri, ws, wko, dskip, wglu),
                          scratch, tile_rows=tile_rows, cast_next=cast_next)


def _causal_conv_rows(val, halo_ref, w_ref, cols=slice(None)):
    tile_rows = val.shape[0]
    ext = jnp.concatenate([halo_ref[:, cols], val], axis=0)
    out = val * w_ref[CONV_WIDTH - 1:CONV_WIDTH, cols]
    for k in range(CONV_WIDTH - 1):
        shift = CONV_WIDTH - 1 - k
        out = out + ext[HALO_ROWS - shift:HALO_ROWS - shift + tile_rows, :] * w_ref[k:k + 1, cols]
    halo_ref[:, cols] = val[tile_rows - HALO_ROWS:, :]
    return out


def _zero_halo_at_sequence_start(halo_ref):
    @pl.when(pl.program_id(1) == 0)
    def _():
        halo_ref[...] = jnp.zeros_like(halo_ref)


def _ffn_kernel(h_ref, gain_ref, wup_ref, cw_ref, cb_ref, wdown_ref, fgain_ref, o_ref,
                halo_ref, *, d_ff, final_norm):
    _zero_halo_at_sequence_start(halo_ref)
    for r in range(h_ref.shape[0] // ROW_BLOCK):
        rows = slice(r * ROW_BLOCK, (r + 1) * ROW_BLOCK)
        x = h_ref[rows, :]
        u = _rmsnorm(x, gain_ref[...]).astype(jnp.bfloat16)
        g = jnp.dot(u, wup_ref[:, :d_ff], preferred_element_type=jnp.float32)
        v = jnp.dot(u, wup_ref[:, d_ff:], preferred_element_type=jnp.float32)
        g = _causal_conv_rows(g, halo_ref, cw_ref) + cb_ref[...]
        act = (jax.nn.silu(g) * v).astype(jnp.bfloat16)
        out = x + jnp.dot(act, wdown_ref[...], preferred_element_type=jnp.float32)
        if final_norm:
            out = _rmsnorm(out, fgain_ref[...])
        o_ref[rows, :] = out


def _ffn_layer(h, gain, wup, cw, cb, wdown, fgain, *, tile_rows, final_norm, cast_next=()):
    d_ff = wdown.shape[0]
    body = functools.partial(_ffn_kernel, d_ff=d_ff, final_norm=final_norm)
    scratch = [pltpu.VMEM((HALO_ROWS, d_ff), jnp.float32)]
    return _sublayer_call(body, "conv_ffn_final" if final_norm else "conv_ffn", h,
                          (gain, wup, cw, cb, wdown, fgain), scratch,
                          tile_rows=tile_rows, cast_next=cast_next)


def _shortconv_kernel(h_ref, gain_ref, win_ref, cw_ref, wout_ref, o_ref, halo_ref):
    _zero_halo_at_sequence_start(halo_ref)
    d = h_ref.shape[-1]
    for r in range(h_ref.shape[0] // SHORTCONV_ROW_BLOCK):
        rows = slice(r * SHORTCONV_ROW_BLOCK, (r + 1) * SHORTCONV_ROW_BLOCK)
        x = h_ref[rows, :]
        u = _rmsnorm(x, gain_ref[...]).astype(jnp.bfloat16)
        gated = []
        for c in range(d // COL_BLOCK):
            cols = slice(c * COL_BLOCK, (c + 1) * COL_BLOCK)
            b_gate, c_gate, hh = [
                jnp.dot(u, win_ref[:, k * d + c * COL_BLOCK:k * d + (c + 1) * COL_BLOCK],
                        preferred_element_type=jnp.float32) for k in range(3)]
            v = _causal_conv_rows(c_gate * hh, halo_ref, cw_ref, cols)
            gated.append((b_gate * v).astype(jnp.bfloat16))
        o_ref[rows, :] = x + jnp.dot(jnp.concatenate(gated, axis=1), wout_ref[...],
                                     preferred_element_type=jnp.float32)


def _shortconv_layer(h, gain, win, cw, wout, *, tile_rows, cast_next):
    scratch = [pltpu.VMEM((HALO_ROWS, h.shape[-1]), jnp.float32)]
    return _sublayer_call(_shortconv_kernel, "shortconv_mixer", h, (gain, win, cw, wout), scratch,
                          tile_rows=tile_rows, cast_next=cast_next)


def _s5_discretise(a_re, a_im, log_dt, b_re, b_im):
    f32 = jnp.float32
    lam_r = a_re.astype(f32)
    lam_i = a_im.astype(f32)
    dt = jnp.exp(log_dt.astype(f32))[:, None]
    mag = jnp.exp(lam_r * dt)
    ab_r = mag * jnp.cos(lam_i * dt)
    ab_i = mag * jnp.sin(lam_i * dt)
    den = lam_r * lam_r + lam_i * lam_i
    nr = ab_r - 1.0
    g_r = ((nr * lam_r + ab_i * lam_i) / den)[..., None]
    g_i = ((ab_i * lam_r - nr * lam_i) / den)[..., None]
    br = b_re.astype(f32)
    bi = b_im.astype(f32)
    return lam_r * dt, lam_i * dt, g_r * br - g_i * bi, g_r * bi + g_i * br


def _a_power(rate_r, rate_i, k):
    k = jnp.asarray(k, jnp.float32)
    mag = jnp.exp(rate_r * k)
    return mag * jnp.cos(rate_i * k), mag * jnp.sin(rate_i * k)


def _cmul(ar, ai, br, bi):
    return ar * br - ai * bi, ar * bi + ai * br


def _s5_tables(rate_r, rate_i, n_chunk_blocks):
    rr = rate_r.reshape(1, -1)
    ri = rate_i.reshape(1, -1)
    mid = CHUNK_BLOCK // 2
    c = jnp.arange(CHUNK_BLOCK)[:, None]
    er, ei = _a_power(rr, ri, LAG * (mid - c))
    fr, fi = _a_power(rr, ri, LAG * (c - 1 - mid))
    pr, pi = _a_power(rr, ri, LAG * CHUNK_BLOCK)
    rep = lambda t: jnp.tile(t.astype(jnp.bfloat16), (n_chunk_blocks, 1))
    return rep(er), rep(ei), rep(fr), rep(fi), pr, pi


def _s5_weights(rate_r, rate_i, bb_r, bb_i, c_re, c_im):
    f32 = jnp.float32
    hp = lax.Precision.HIGHEST
    n_groups, n_state, n_ch = bb_r.shape
    gpp = PAIR_CH // n_ch
    n_pairs = n_groups // gpp
    wide = LAG * PAIR_CH
    pair_modes = gpp * n_state
    lane = jnp.arange(wide)
    lane_lag, lane_g2, lane_ch = lane // PAIR_CH, (lane // n_ch) % gpp, lane % n_ch
    mode_g2 = jnp.arange(pair_modes) // n_state
    ch = jnp.arange(n_ch)
    lags = jnp.arange(LAG)
    spread = (ch[:, None] == lane_ch[None, :]).astype(f32)
    lag_spread = (lags[:, None] == lane_lag[None, :]).astype(f32)
    own = (mode_g2[:, None] == lane_g2[None, :]).astype(f32)
    causal = (lane_lag[None, :] >= lane_lag[:, None]).astype(f32)
    pr = rate_r.reshape(n_pairs, pair_modes)
    pi = rate_i.reshape(n_pairs, pair_modes)

    def pair_rows(w):
        return w.reshape((n_pairs, pair_modes) + w.shape[2:])

    cw_r = jnp.einsum('gop,oc->gpc', c_re.astype(f32), spread, precision=hp)
    cw_i = jnp.einsum('gop,oc->gpc', c_im.astype(f32), spread, precision=hp)
    ad_r, ad_i = [jnp.einsum('kqd,dc->kqc', t, lag_spread, precision=hp)
                  for t in _a_power(pr[..., None], pi[..., None], lags)]
    rout_r, rout_i = _cmul(ad_r, ad_i, pair_rows(cw_r) * own, pair_rows(cw_i) * own)
    bt_r = jnp.einsum('hc,kqh->kcq', spread, pair_rows(bb_r), precision=hp)
    bt_i = jnp.einsum('hc,kqh->kcq', spread, pair_rows(bb_i), precision=hp)
    an_r, an_i = [jnp.repeat(t, PAIR_CH, axis=1)
                  for t in _a_power(pr[:, None, :], pi[:, None, :], -lags[None, :, None])]
    lin_r, lin_i = _cmul(an_r, an_i, bt_r * own.T, bt_i * own.T)

    wk = (jnp.einsum('kcq,kqe->kce', lin_r, rout_r, precision=hp)
          - jnp.einsum('kcq,kqe->kce', lin_i, rout_i, precision=hp)) * causal
    ws_r, ws_i = _cmul(*_a_power(pr[:, None, :], pi[:, None, :], LAG - 1), lin_r, lin_i)
    wo_r, wo_i = _cmul(*_a_power(pr[..., None], pi[..., None], 1), rout_r, rout_i)

    ws = jnp.concatenate([ws_r, ws_i], axis=2)
    wko = jnp.concatenate([wk, wo_r, -wo_i], axis=1)
    return ws.astype(jnp.bfloat16), wko.astype(jnp.bfloat16)


def _chunk_prefix_matrix(n_chunk_blocks):
    r = jnp.arange(CHUNK_BLOCK + HALO_ROWS)[:, None]
    c = jnp.arange(CHUNK_BLOCK)[None, :]
    one_block = ((c < r) & (r <= CHUNK_BLOCK)).astype(jnp.float32)
    return jnp.kron(jnp.eye(n_chunk_blocks, dtype=jnp.float32), one_block).astype(jnp.bfloat16)


def kernel(x, norm_mix, norm_ffn, norm_final, s5_a_re, s5_a_im, s5_log_dt, s5_b_re, s5_b_im,
           s5_c_re, s5_c_im, s5_d, s5_w_glu, sc_w_in, sc_conv_w, sc_w_out,
           ffn_w_up, ffn_conv_w, ffn_conv_b, ffn_w_down):
    bf16 = jnp.bfloat16
    row = lambda v: v.reshape(1, -1)

    rate_r, rate_i, bb_r, bb_i = _s5_discretise(s5_a_re[0], s5_a_im[0], s5_log_dt[0],
                                                 s5_b_re[0], s5_b_im[0])
    n_chunk_blocks = TILE_ROWS // (LAG * CHUNK_BLOCK)
    tabs = _s5_tables(rate_r, rate_i, n_chunk_blocks)
    ws, wko = _s5_weights(rate_r, rate_i, bb_r, bb_i, s5_c_re[0], s5_c_im[0])
    h, (wup0, wdown0) = _s5_layer(
        x, row(norm_mix[0]), tabs, _chunk_prefix_matrix(n_chunk_blocks), ws, wko, row(s5_d[0]),
        s5_w_glu[0].astype(bf16), tile_rows=TILE_ROWS,
        cast_next=((ffn_w_up, 0), (ffn_w_down, 0)))
    h, (win, wout) = _ffn_layer(
        h, row(norm_ffn[0]), wup0, ffn_conv_w[0], row(ffn_conv_b[0]), wdown0, row(norm_final),
        tile_rows=TILE_ROWS, final_norm=False, cast_next=((sc_w_in, 0), (sc_w_out, 0)))
    h, (wup1, wdown1) = _shortconv_layer(
        h, row(norm_mix[1]), win, sc_conv_w[0], wout, tile_rows=2 * TILE_ROWS,
        cast_next=((ffn_w_up, 1), (ffn_w_down, 1)))
    h, _ = _ffn_layer(
        h, row(norm_ffn[1]), wup1, ffn_conv_w[1], row(ffn_conv_b[1]), wdown1, row(norm_final),
        tile_rows=TILE_ROWS, final_norm=True)
    return h
```

```python
import functools

import jax
import jax.numpy as jnp
from jax import lax
from jax.experimental import pallas as pl
from jax.experimental.pallas import tpu as pltpu

RMS_EPS = 1e-6
S5_GROUP = 16
S5_STATE = 64
CONV_WIDTH = 3
HALO_ROWS = 8

LANES = 128
BF16_SUBLANES = 16
LAG = 8
PAIR_CH = 32
CHUNK_BLOCK = 16
COL_BLOCK = 256
ROW_BLOCK = 256
SHORTCONV_ROW_BLOCK = 1024
TILE_ROWS = 1024
S5_ROW_PARTS = 2
VMEM_LIMIT_BYTES = 56 * 1024 * 1024


def _rmsnorm(x, gain):
    ms = jnp.mean(x * x, axis=-1, keepdims=True)
    return x * lax.rsqrt(ms + RMS_EPS) * gain


def _pack_rows(v):
    return pltpu.bitcast(v.astype(jnp.bfloat16), jnp.uint32)


def _const_spec(shape):
    nd = len(shape)
    return pl.BlockSpec(shape, lambda b, i: (0,) * nd, pipeline_mode=pl.Buffered(1))


def _tile_spec(tile_rows, d):
    return pl.BlockSpec((None, tile_rows, d), lambda b, i: (b, i, 0))


def _compiler_params():
    return pltpu.CompilerParams(
        dimension_semantics=("arbitrary", "arbitrary"),
        vmem_limit_bytes=VMEM_LIMIT_BYTES,
    )


def _cast_blocks(n_rows, n_steps):
    for n_blocks in range(n_steps, 0, -1):
        if (n_steps % n_blocks == 0 and n_rows % n_blocks == 0
                and (n_rows // n_blocks) % BF16_SUBLANES == 0):
            return n_blocks
    raise ValueError(f"no row blocking of {n_rows} rows over {n_steps} steps")


def _sublayer_call(body, name, h, consts, scratch_shapes, *, tile_rows, cast_next=()):
    bsz, seq, d = h.shape
    n_tiles = seq // tile_rows
    n_steps = bsz * n_tiles
    n_in = 1 + len(consts)
    n_cast = len(cast_next)

    def cast_specs(w, layer):
        _, n_rows, n_cols = w.shape
        n_blocks = _cast_blocks(n_rows, n_steps)
        steps_per_block = n_steps // n_blocks
        block = lambda b, i: (b * n_tiles + i) // steps_per_block
        return (pl.BlockSpec((None, n_rows // n_blocks, n_cols), lambda b, i: (layer, block(b, i), 0)),
                pl.BlockSpec((n_rows // n_blocks, n_cols), lambda b, i: (block(b, i), 0)))

    def kernel(*refs):
        ins, cast_ins = refs[:n_in], refs[n_in:n_in + n_cast]
        out, cast_outs = refs[n_in + n_cast], refs[n_in + n_cast + 1:n_in + 2 * n_cast + 1]
        scratch = refs[n_in + 2 * n_cast + 1:]
        for src, dst in zip(cast_ins, cast_outs):
            dst[...] = src[...].astype(dst.dtype)
        body(*ins, out, *scratch)

    cast_in, cast_out = zip(*[cast_specs(w, layer) for w, layer in cast_next]) if cast_next else ((), ())
    outs = pl.pallas_call(
        kernel,
        grid=(bsz, n_tiles),
        in_specs=[_tile_spec(tile_rows, d)] + [_const_spec(c.shape) for c in consts] + list(cast_in),
        out_specs=[_tile_spec(tile_rows, d)] + list(cast_out),
        out_shape=[jax.ShapeDtypeStruct(h.shape, h.dtype)]
        + [jax.ShapeDtypeStruct(w.shape[1:], jnp.bfloat16) for w, _ in cast_next],
        scratch_shapes=scratch_shapes,
        compiler_params=_compiler_params(),
        name=name,
    )(h, *consts, *[w for w, _ in cast_next])
    return outs[0], tuple(outs[1:])


def _s5_kernel(h_ref, er_ref, ei_ref, fr_ref, fi_ref, pr_ref, pi_ref,
               tri_ref, ws_ref, wko_ref, dskip_ref, wglu_ref, o_ref,
               carry_ref, slab_ref, hp_ref):
    @pl.when(pl.program_id(1) == 0)
    def _():
        carry_ref[...] = jnp.zeros_like(carry_ref)

    bf16 = jnp.bfloat16
    f32 = jnp.float32
    tile_rows, d = h_ref.shape
    n_lane_tiles = d // LANES
    pairs_per_lane_tile = LANES // PAIR_CH
    n_pairs = d // PAIR_CH
    pair_modes = (PAIR_CH // S5_GROUP) * S5_STATE
    n_chunks = tile_rows // LAG
    n_cblk = n_chunks // CHUNK_BLOCK
    cb_rows = CHUNK_BLOCK + HALO_ROWS

    x = h_ref[...]
    u = _rmsnorm(x, 1.0)
    part_rows = tile_rows // S5_ROW_PARTS
    part_chunks = n_chunks // S5_ROW_PARTS
    for hf in range(S5_ROW_PARTS):
        for j in range(n_lane_tiles):
            slab_ref[hf, j] = u[hf * part_rows:(hf + 1) * part_rows, j * LANES:(j + 1) * LANES]
    lag_rows = [[_pack_rows(jnp.concatenate(
        [slab_ref[hf, j, pl.ds(dd, part_chunks, stride=LAG), :] for hf in range(S5_ROW_PARTS)], axis=0))
        for j in range(n_lane_tiles)] for dd in range(LAG)]
    u2, s_re, s_im = [], [], []
    for k in range(n_pairs):
        j, q = divmod(k, pairs_per_lane_tile)
        lanes = slice(q * PAIR_CH, (q + 1) * PAIR_CH)
        u2k = pltpu.bitcast(
            jnp.concatenate([lag_rows[dd][j][:, lanes] for dd in range(LAG)], axis=1), bf16)
        u2.append(u2k)
        sk = jnp.dot(u2k, ws_ref[k], preferred_element_type=f32)
        s_re.append(sk[:, :pair_modes])
        s_im.append(sk[:, pair_modes:])
    sr = jnp.concatenate(s_re, axis=1).astype(bf16)
    si = jnp.concatenate(s_im, axis=1).astype(bf16)

    er, ei, fr, fi = er_ref[...], ei_ref[...], fr_ref[...], fi_ref[...]
    pr, pi = pr_ref[...], pi_ref[...]
    xr = jnp.dot(tri_ref[...], er * sr - ei * si, preferred_element_type=f32)
    xi = jnp.dot(tri_ref[...], er * si + ei * sr, preferred_element_type=f32)
    c_r, c_i = carry_ref[0:1, :], carry_ref[1:2, :]
    cs_r, cs_i = [], []
    for blk in range(n_cblk):
        cs_r.append(c_r)
        cs_i.append(c_i)
        tot = blk * cb_rows + CHUNK_BLOCK
        t_r = xr[tot:tot + 1, :] + c_r
        t_i = xi[tot:tot + 1, :] + c_i
        c_r, c_i = pr * t_r - pi * t_i, pr * t_i + pi * t_r
    carry_ref[0:1, :] = c_r
    carry_ref[1:2, :] = c_i

    def with_carry(v, cs):
        return jnp.concatenate(
            [v[blk * cb_rows:blk * cb_rows + CHUNK_BLOCK, :] + cs[blk] for blk in range(n_cblk)],
            axis=0).astype(bf16)

    gr, gi = with_carry(xr, cs_r), with_carry(xi, cs_i)
    hpr = fr * gr - fi * gi
    hpi = fr * gi + fi * gr
    for k in range(n_pairs):
        md = slice(k * pair_modes, (k + 1) * pair_modes)
        hp_ref[:, 2 * k * pair_modes:(2 * k + 1) * pair_modes] = hpr[:, md]
        hp_ref[:, (2 * k + 1) * pair_modes:(2 * k + 2) * pair_modes] = hpi[:, md]

    y2 = []
    for k in range(n_pairs):
        hp = hp_ref[:, 2 * k * pair_modes:(2 * k + 2) * pair_modes]
        y2.append(_pack_rows(jnp.dot(jnp.concatenate([u2[k], hp], axis=1), wko_ref[k],
                                     preferred_element_type=f32)))
    for dd in range(LAG):
        lanes = slice(dd * PAIR_CH, (dd + 1) * PAIR_CH)
        for j in range(n_lane_tiles):
            piece = jnp.concatenate(
                [y2[j * pairs_per_lane_tile + q][:, lanes] for q in range(pairs_per_lane_tile)],
                axis=1)
            rows_f32 = pltpu.bitcast(piece, bf16).astype(f32)
            for hf in range(S5_ROW_PARTS):
                slab_ref[hf, j, pl.ds(dd, part_chunks, stride=LAG), :] = (
                    rows_f32[hf * part_chunks:(hf + 1) * part_chunks, :])

    for hf in range(S5_ROW_PARTS):
        rows = slice(hf * part_rows, (hf + 1) * part_rows)
        y = jnp.concatenate([slab_ref[hf, j] for j in range(n_lane_tiles)], axis=1)
        z = jax.nn.gelu(y + dskip_ref[...] * u[rows, :]).astype(bf16)
        for c in range(d // COL_BLOCK):
            cols = slice(c * COL_BLOCK, (c + 1) * COL_BLOCK)
            gcols = slice(d + c * COL_BLOCK, d + (c + 1) * COL_BLOCK)
            za = jnp.dot(z, wglu_ref[:, cols], preferred_element_type=f32)
            zg = jnp.dot(z, wglu_ref[:, gcols], preferred_element_type=f32)
            o_ref[rows, cols] = x[rows, cols] + za * jax.nn.sigmoid(zg)


def _s5_layer(h, tabs, tri, ws, wko, dskip, wglu, *, tile_rows, cast_next):
    d = h.shape[-1]
    n_modes = tabs[0].shape[1]
    scratch = [
        pltpu.VMEM((HALO_ROWS, n_modes), jnp.float32),
        pltpu.VMEM((S5_ROW_PARTS, d // LANES, tile_rows // S5_ROW_PARTS, LANES), jnp.float32),
        pltpu.VMEM((tile_rows // LAG, 2 * n_modes), jnp.bfloat16),
    ]
    return _sublayer_call(_s5_kernel, "s5_mixer", h, (*tabs, tri, ws, wko, dskip, wglu),
                          scratch, tile_rows=tile_rows, cast_next=cast_next)


def _causal_conv_rows(val, halo_ref, w_ref, cols=slice(None)):
    tile_rows = val.shape[0]
    ext = jnp.concatenate([halo_ref[:, cols], val], axis=0)
    out = val * w_ref[CONV_WIDTH - 1:CONV_WIDTH, cols]
    for k in range(CONV_WIDTH - 1):
        shift = CONV_WIDTH - 1 - k
        out = out + ext[HALO_ROWS - shift:HALO_ROWS - shift + tile_rows, :] * w_ref[k:k + 1, cols]
    halo_ref[:, cols] = val[tile_rows - HALO_ROWS:, :]
    return out


def _zero_halo_at_sequence_start(halo_ref):
    @pl.when(pl.program_id(1) == 0)
    def _():
        halo_ref[...] = jnp.zeros_like(halo_ref)


def _ffn_kernel(h_ref, gain_ref, wup_ref, cw_ref, cb_ref, wdown_ref, fgain_ref, o_ref,
                halo_ref, *, d_ff, final_norm):
    _zero_halo_at_sequence_start(halo_ref)
    for r in range(h_ref.shape[0] // ROW_BLOCK):
        rows = slice(r * ROW_BLOCK, (r + 1) * ROW_BLOCK)
        x = h_ref[rows, :]
        u = _rmsnorm(x, gain_ref[...]).astype(jnp.bfloat16)
        g = jnp.dot(u, wup_ref[:, :d_ff], preferred_element_type=jnp.float32)
        v = jnp.dot(u, wup_ref[:, d_ff:], preferred_element_type=jnp.float32)
        g = _causal_conv_rows(g, halo_ref, cw_ref) + cb_ref[...]
        act = (jax.nn.silu(g) * v).astype(jnp.bfloat16)
        out = x + jnp.dot(act, wdown_ref[...], preferred_element_type=jnp.float32)
        if final_norm:
            out = _rmsnorm(out, fgain_ref[...])
        o_ref[rows, :] = out


def _ffn_layer(h, gain, wup, cw, cb, wdown, fgain, *, tile_rows, final_norm, cast_next=()):
    d_ff = wdown.shape[0]
    body = functools.partial(_ffn_kernel, d_ff=d_ff, final_norm=final_norm)
    scratch = [pltpu.VMEM((HALO_ROWS, d_ff), jnp.float32)]
    return _sublayer_call(body, "conv_ffn_final" if final_norm else "conv_ffn", h,
                          (gain, wup, cw, cb, wdown, fgain), scratch,
                          tile_rows=tile_rows, cast_next=cast_next)


def _shortconv_kernel(h_ref, gain_ref, win_ref, cw_ref, wout_ref, o_ref, halo_ref):
    _zero_halo_at_sequence_start(halo_ref)
    d = h_ref.shape[-1]
    for r in range(h_ref.shape[0] // SHORTCONV_ROW_BLOCK):
        rows = slice(r * SHORTCONV_ROW_BLOCK, (r + 1) * SHORTCONV_ROW_BLOCK)
        x = h_ref[rows, :]
        u = _rmsnorm(x, gain_ref[...]).astype(jnp.bfloat16)
        gated = []
        for c in range(d // COL_BLOCK):
            cols = slice(c * COL_BLOCK, (c + 1) * COL_BLOCK)
            b_gate, c_gate, hh = [
                jnp.dot(u, win_ref[:, k * d + c * COL_BLOCK:k * d + (c + 1) * COL_BLOCK],
                        preferred_element_type=jnp.float32) for k in range(3)]
            v = _causal_conv_rows(c_gate * hh, halo_ref, cw_ref, cols)
            gated.append((b_gate * v).astype(jnp.bfloat16))
        o_ref[rows, :] = x + jnp.dot(jnp.concatenate(gated, axis=1), wout_ref[...],
                                     preferred_element_type=jnp.float32)


def _shortconv_layer(h, gain, win, cw, wout, *, tile_rows, cast_next):
    scratch = [pltpu.VMEM((HALO_ROWS, h.shape[-1]), jnp.float32)]
    return _sublayer_call(_shortconv_kernel, "shortconv_mixer", h, (gain, win, cw, wout), scratch,
                          tile_rows=tile_rows, cast_next=cast_next)


def _s5_discretise(a_re, a_im, log_dt, b_re, b_im):
    f32 = jnp.float32
    lam_r = a_re.astype(f32)
    lam_i = a_im.astype(f32)
    dt = jnp.exp(log_dt.astype(f32))[:, None]
    mag = jnp.exp(lam_r * dt)
    ab_r = mag * jnp.cos(lam_i * dt)
    ab_i = mag * jnp.sin(lam_i * dt)
    den = lam_r * lam_r + lam_i * lam_i
    nr = ab_r - 1.0
    g_r = ((nr * lam_r + ab_i * lam_i) / den)[..., None]
    g_i = ((ab_i * lam_r - nr * lam_i) / den)[..., None]
    br = b_re.astype(f32)
    bi = b_im.astype(f32)
    return lam_r * dt, lam_i * dt, g_r * br - g_i * bi, g_r * bi + g_i * br


def _a_power(rate_r, rate_i, k):
    k = jnp.asarray(k, jnp.float32)
    mag = jnp.exp(rate_r * k)
    return mag * jnp.cos(rate_i * k), mag * jnp.sin(rate_i * k)


def _cmul(ar, ai, br, bi):
    return ar * br - ai * bi, ar * bi + ai * br


def _s5_tables(rate_r, rate_i, n_chunk_blocks):
    rr = rate_r.reshape(1, -1)
    ri = rate_i.reshape(1, -1)
    mid = CHUNK_BLOCK // 2
    c = jnp.arange(CHUNK_BLOCK)[:, None]
    er, ei = _a_power(rr, ri, LAG * (mid - c))
    fr, fi = _a_power(rr, ri, LAG * (c - 1 - mid))
    pr, pi = _a_power(rr, ri, LAG * CHUNK_BLOCK)
    rep = lambda t: jnp.tile(t.astype(jnp.bfloat16), (n_chunk_blocks, 1))
    return rep(er), rep(ei), rep(fr), rep(fi), pr, pi


def _s5_weights(rate_r, rate_i, bb_r, bb_i, c_re, c_im, gain):
    f32 = jnp.float32
    hp = lax.Precision.HIGHEST
    n_groups, n_state, n_ch = bb_r.shape
    gpp = PAIR_CH // n_ch
    n_pairs = n_groups // gpp
    wide = LAG * PAIR_CH
    pair_modes = gpp * n_state
    lane = jnp.arange(wide)
    lane_lag, lane_g2, lane_ch = lane // PAIR_CH, (lane // n_ch) % gpp, lane % n_ch
    mode_g2 = jnp.arange(pair_modes) // n_state
    ch = jnp.arange(n_ch)
    lags = jnp.arange(LAG)
    spread = (ch[:, None] == lane_ch[None, :]).astype(f32)
    lag_spread = (lags[:, None] == lane_lag[None, :]).astype(f32)
    own = (mode_g2[:, None] == lane_g2[None, :]).astype(f32)
    causal = (lane_lag[None, :] >= lane_lag[:, None]).astype(f32)
    pr = rate_r.reshape(n_pairs, pair_modes)
    pi = rate_i.reshape(n_pairs, pair_modes)

    def pair_rows(w):
        return w.reshape((n_pairs, pair_modes) + w.shape[2:])

    cw_r = jnp.einsum('gop,oc->gpc', c_re.astype(f32), spread, precision=hp)
    cw_i = jnp.einsum('gop,oc->gpc', c_im.astype(f32), spread, precision=hp)
    ad_r, ad_i = [jnp.einsum('kqd,dc->kqc', t, lag_spread, precision=hp)
                  for t in _a_power(pr[..., None], pi[..., None], lags)]
    rout_r, rout_i = _cmul(ad_r, ad_i, pair_rows(cw_r) * own, pair_rows(cw_i) * own)
    bt_r = jnp.einsum('hc,kqh->kcq', spread, pair_rows(bb_r), precision=hp)
    bt_i = jnp.einsum('hc,kqh->kcq', spread, pair_rows(bb_i), precision=hp)
    an_r, an_i = [jnp.repeat(t, PAIR_CH, axis=1)
                  for t in _a_power(pr[:, None, :], pi[:, None, :], -lags[None, :, None])]
    gain_rows = jnp.tile(gain.astype(f32).reshape(n_pairs, PAIR_CH), (1, LAG))[:, :, None]
    lin_r, lin_i = _cmul(an_r, an_i, bt_r * own.T * gain_rows, bt_i * own.T * gain_rows)

    wk = (jnp.einsum('kcq,kqe->kce', lin_r, rout_r, precision=hp)
          - jnp.einsum('kcq,kqe->kce', lin_i, rout_i, precision=hp)) * causal
    ws_r, ws_i = _cmul(*_a_power(pr[:, None, :], pi[:, None, :], LAG - 1), lin_r, lin_i)
    wo_r, wo_i = _cmul(*_a_power(pr[..., None], pi[..., None], 1), rout_r, rout_i)

    ws = jnp.concatenate([ws_r, ws_i], axis=2)
    wko = jnp.concatenate([wk, wo_r, -wo_i], axis=1)
    return ws.astype(jnp.bfloat16), wko.astype(jnp.bfloat16)


def _chunk_prefix_matrix(n_chunk_blocks):
    r = jnp.arange(CHUNK_BLOCK + HALO_ROWS)[:, None]
    c = jnp.arange(CHUNK_BLOCK)[None, :]
    one_block = ((c < r) & (r <= CHUNK_BLOCK)).astype(jnp.float32)
    return jnp.kron(jnp.eye(n_chunk_blocks, dtype=jnp.float32), one_block).astype(jnp.bfloat16)


def kernel(x, norm_mix, norm_ffn, norm_final, s5_a_re, s5_a_im, s5_log_dt, s5_b_re, s5_b_im,
           s5_c_re, s5_c_im, s5_d, s5_w_glu, sc_w_in, sc_conv_w, sc_w_out,
           ffn_w_up, ffn_conv_w, ffn_conv_b, ffn_w_down):
    bf16 = jnp.bfloat16
    row = lambda v: v.reshape(1, -1)

    rate_r, rate_i, bb_r, bb_i = _s5_discretise(s5_a_re[0], s5_a_im[0], s5_log_dt[0],
                                                 s5_b_re[0], s5_b_im[0])
    n_chunk_blocks = TILE_ROWS // (LAG * CHUNK_BLOCK)
    tabs = _s5_tables(rate_r, rate_i, n_chunk_blocks)
    ws, wko = _s5_weights(rate_r, rate_i, bb_r, bb_i, s5_c_re[0], s5_c_im[0], norm_mix[0])
    h, (wup0, wdown0) = _s5_layer(
        x, tabs, _chunk_prefix_matrix(n_chunk_blocks), ws, wko, row(s5_d[0] * norm_mix[0]),
        s5_w_glu[0].astype(bf16), tile_rows=TILE_ROWS,
        cast_next=((ffn_w_up, 0), (ffn_w_down, 0)))
    h, (win, wout) = _ffn_layer(
        h, row(norm_ffn[0]), wup0, ffn_conv_w[0], row(ffn_conv_b[0]), wdown0, row(norm_final),
        tile_rows=TILE_ROWS, final_norm=False, cast_next=((sc_w_in, 0), (sc_w_out, 0)))
    h, (wup1, wdown1) = _shortconv_layer(
        h, row(norm_mix[1]), win, sc_conv_w[0], wout, tile_rows=2 * TILE_ROWS,
        cast_next=((ffn_w_up, 1), (ffn_w_down, 1)))
    h, _ = _ffn_layer(
        h, row(norm_ffn[1]), wup1, ffn_conv_w[1], row(ffn_conv_b[1]), wdown1, row(norm_final),
        tile_rows=TILE_ROWS, final_norm=True)
    return h
```
